```python
import math
import jax, jax.numpy as jnp
from jax import lax
import numpy as np

D_MODEL = 1024
BATCH = 8
SEQ = 8192
DEPTH = 1

GRID_W = 64
CTX_LEN = 256
N_MOD = 6
DIFF_HEADS = 4
DIFF_DH = 64
DIFF_VD = 2 * DIFF_DH
ATTN_WIDTH = DIFF_HEADS * DIFF_VD
HY_WIDTH = D_MODEL - ATTN_WIDTH
HY_ORDER = 2
HY_DIRS = 2
HY_EMB_DIM = 33
HY_BANDS = (HY_EMB_DIM - 1) // 2
HY_FILTER_ORDER = 64
HY_DECAY_TARGET = 1e-2
HY_FAST_DECAY_PCT = 0.3
HY_SLOW_DECAY_PCT = 1.5
Q_COLS = DIFF_HEADS * 2 * DIFF_DH
K_COLS = Q_COLS
V_COLS = ATTN_WIDTH
HY_COLS = (HY_ORDER + 1) * HY_WIDTH
IN_COLS = Q_COLS + K_COLS + V_COLS + HY_COLS
MIX_WIDTH = ATTN_WIDTH + HY_WIDTH
D_FF = 2816
SHORT_CONV = 3
ROPE_BASE = 10000.0
QBLOCK = 128
EPS = 1e-6

kernel_name = "hybrid_diffattn_hyena_dit_layer"

F32 = jnp.float32


def rmsnorm(x, g):
    xf = x.astype(F32)
    y = xf * lax.rsqrt(jnp.mean(xf * xf, axis=-1, keepdims=True) + EPS) * g.astype(F32)
    return y.astype(x.dtype)


def modulate(h, shift, scale):
    return h * (1.0 + scale) + shift


def split_mod(mod):
    m = mod.reshape(mod.shape[:-1] + (1, mod.shape[-1]))
    return jnp.split(m, N_MOD, axis=-1)


def short_conv(x, w, b):
    L = x.shape[1]
    pad = SHORT_CONV // 2
    xp = jnp.pad(x, ((0, 0), (pad, pad), (0, 0)))
    return sum(xp[:, i:i + L] * w[i] for i in range(SHORT_CONV)) + b


def heads_qk(t):
    B, L, _ = t.shape
    return t.reshape(B, L, DIFF_HEADS, 2, DIFF_DH).transpose(0, 2, 3, 1, 4)


def heads_v(t):
    B, L, _ = t.shape
    return t.reshape(B, L, DIFF_HEADS, DIFF_VD).transpose(0, 2, 1, 3)


def axial_rope(t, row, col):
    half = DIFF_DH // 2
    nf = half // 2
    inv = ROPE_BASE ** (-jnp.arange(nf, dtype=F32) / nf)

    def rot(part, pos):
        ang = pos.astype(F32)[:, None] * inv
        cos, sin = jnp.cos(ang), jnp.sin(ang)
        p1 = part[..., :nf].astype(F32)
        p2 = part[..., nf:].astype(F32)
        return jnp.concatenate([p1 * cos - p2 * sin, p2 * cos + p1 * sin], axis=-1)

    out = jnp.concatenate([rot(t[..., :half], row), rot(t[..., half:], col)], axis=-1)
    return out.astype(t.dtype)


def diff_maps(q, k, v, lam):
    s = jnp.einsum("bhmqd,bhmkd->bhmqk", q, k, preferred_element_type=F32) * (DIFF_DH ** -0.5)
    a = jax.nn.softmax(s, axis=-1)
    w = a[:, :, 0] - lam * a[:, :, 1]
    return jnp.einsum("bhqk,bhkd->bhqd", w.astype(v.dtype), v)


def diff_attn_blocked(q, k, v, lam):
    B, H, M, S, DH = q.shape
    nb = S // QBLOCK
    qb = jnp.moveaxis(q.reshape(B, H, M, nb, QBLOCK, DH), 3, 0)
    o = lax.map(lambda qi: diff_maps(qi, k, v, lam), qb)
    return jnp.moveaxis(o, 0, 2).reshape(B, H, S, DIFF_VD)


def diff_out(o, subln_g, lam_init):
    o = rmsnorm(o, subln_g) * (1.0 - lam_init)
    B, H, L, VD = o.shape
    return o.transpose(0, 2, 1, 3).reshape(B, L, H * VD)


def hyena_filters(L, p):
    t = jnp.linspace(0.0, 1.0, L, dtype=F32)[:, None]
    w = 2.0 * math.pi * jnp.arange(L, dtype=F32)[:, None] / L
    f = jnp.linspace(1e-4, HY_BANDS - 1, HY_BANDS, dtype=F32)[None, :]
    z = jnp.concatenate([t, jnp.cos(f * w), -jnp.sin(f * w)], axis=-1)
    freq = p["hy_freq"].astype(F32)
    h = jnp.sin(freq * (z @ p["hy_w1"].astype(F32) + p["hy_b1"].astype(F32)))
    h = jnp.sin(freq * (h @ p["hy_w2"].astype(F32) + p["hy_b2"].astype(F32)))
    h = jnp.sin(freq * (h @ p["hy_w3"].astype(F32) + p["hy_b3"].astype(F32)))
    h = h @ p["hy_w4"].astype(F32)
    min_decay = math.log(HY_DECAY_TARGET) / HY_FAST_DECAY_PCT
    max_decay = math.log(HY_DECAY_TARGET) / HY_SLOW_DECAY_PCT
    deltas = jnp.abs(jnp.linspace(min_decay, max_decay, h.shape[-1], dtype=F32))
    h = h * jnp.exp(-t * deltas)
    h = h.reshape(L, HY_ORDER, HY_DIRS, HY_WIDTH)
    k_full = jnp.concatenate(
        [h[:, :, 0], jnp.zeros((1, HY_ORDER, HY_WIDTH), F32), h[:0:-1, :, 1]], axis=0)
    return jnp.fft.rfft(k_full, axis=0)


def long_conv(z, k_f):
    L = z.shape[1]
    zf = jnp.fft.rfft(z.astype(F32), n=2 * L, axis=1)
    y = jnp.fft.irfft(zf * k_f[None], n=2 * L, axis=1)[:, :L]
    return y.astype(z.dtype)


def hyena_mixer(proj, p, k_f):
    u = short_conv(proj, p["hy_conv_w"], p["hy_conv_b"])
    v, x1, x2 = jnp.split(u, HY_ORDER + 1, axis=-1)
    z = v
    for n, gate in enumerate((x1, x2)):
        z = gate * (long_conv(z, k_f[:, n]) + p["hy_skip"][n] * z)
    return rmsnorm(z, p["hy_norm"])


def conv_ffn(h, p):
    a, b = jnp.split(h @ p["ffn_w_up"], 2, axis=-1)
    a = short_conv(a, p["ffn_conv_w"], p["ffn_conv_b"])
    return (jax.nn.gelu(a, approximate=False) * b) @ p["ffn_w_down"]


def trunk_layer(x, ctx, mod_x, mod_c, p, layer_idx, update_ctx):
    L = x.shape[1]
    rows = L // GRID_W
    row = jnp.repeat(jnp.arange(rows, dtype=jnp.int32), GRID_W)
    col = jnp.tile(jnp.arange(GRID_W, dtype=jnp.int32), rows)
    sh_a, sc_a, g_a, sh_f, sc_f, g_f = split_mod(mod_x)
    csh_a, csc_a, cg_a, csh_f, csc_f, cg_f = split_mod(mod_c)

    lam_init = 0.8 - 0.6 * math.exp(-0.3 * layer_idx)
    lam = (jnp.exp(jnp.sum(p["lam_q1"].astype(F32) * p["lam_k1"].astype(F32)))
           - jnp.exp(jnp.sum(p["lam_q2"].astype(F32) * p["lam_k2"].astype(F32)))
           + lam_init)
    cuts = [Q_COLS, Q_COLS + K_COLS, Q_COLS + K_COLS + V_COLS]
    w_q, w_k, w_v, w_hy = jnp.split(p["w_in"], cuts, axis=1)

    hx = modulate(rmsnorm(x, p["norm_mix"]), sh_a, sc_a)
    hc = modulate(rmsnorm(ctx, p["norm_mix"]), csh_a, csc_a)
    q_x, k_x, v_x, hy_x = jnp.split(hx @ p["w_in"], cuts, axis=-1)
    k_c = heads_qk(hc @ w_k)
    v_c = heads_v(hc @ w_v)
    q_x = axial_rope(heads_qk(q_x), row, col)
    k_x = axial_rope(heads_qk(k_x), row, col)
    k_all = jnp.concatenate([k_c, k_x], axis=3)
    v_all = jnp.concatenate([v_c, heads_v(v_x)], axis=2)
    attn_x = diff_out(diff_attn_blocked(q_x, k_all, v_all, lam), p["subln"], lam_init)
    hyo_x = hyena_mixer(hy_x, p, hyena_filters(L, p))
    x = x + g_a * (jnp.concatenate([attn_x, hyo_x], axis=-1) @ p["w_out"])

    x = x + g_f * conv_ffn(modulate(rmsnorm(x, p["norm_ffn"]), sh_f, sc_f), p)

    if update_ctx:
        attn_c = diff_out(diff_maps(heads_qk(hc @ w_q), k_c, v_c, lam), p["subln"], lam_init)
        hyo_c = hyena_mixer(hc @ w_hy, p, hyena_filters(ctx.shape[1], p))
        ctx = ctx + cg_a * (jnp.concatenate([attn_c, hyo_c], axis=-1) @ p["w_out"])
        ctx = ctx + cg_f * conv_ffn(modulate(rmsnorm(ctx, p["norm_ffn"]), csh_f, csc_f), p)
    return x, ctx


def setup_inputs(seed: int = 0) -> dict:
    key = jax.random.key(seed)
    ks = jax.random.split(key, 32)

    def nrm(k, shape, scale):
        return jax.random.normal(k, shape, F32) * scale

    def gain(k, shape):
        return 1.0 + nrm(k, shape, 0.02)

    return {
        "x": nrm(ks[0], (BATCH, SEQ, D_MODEL), 1.0),
        "c": nrm(ks[1], (BATCH, D_MODEL), 1.0),
        "ctx": nrm(ks[2], (BATCH, CTX_LEN, D_MODEL), 1.0),
        "c_ctx": nrm(ks[3], (D_MODEL,), 1.0),
        "w_mod": nrm(ks[4], (DEPTH, D_MODEL, N_MOD * D_MODEL), 0.3 * D_MODEL ** -0.5),
        "b_mod": nrm(ks[5], (DEPTH, N_MOD * D_MODEL), 0.01),
        "norm_mix": gain(ks[6], (DEPTH, D_MODEL)),
        "norm_ffn": gain(ks[7], (DEPTH, D_MODEL)),
        "w_in": nrm(ks[8], (DEPTH, D_MODEL, IN_COLS), D_MODEL ** -0.5),
        "lam_q1": nrm(ks[9], (DEPTH, DIFF_DH), 0.1),
        "lam_k1": nrm(ks[10], (DEPTH, DIFF_DH), 0.1),
        "lam_q2": nrm(ks[11], (DEPTH, DIFF_DH), 0.1),
        "lam_k2": nrm(ks[12], (DEPTH, DIFF_DH), 0.1),
        "subln": gain(ks[13], (DEPTH, DIFF_VD)),
        "hy_conv_w": nrm(ks[14], (DEPTH, SHORT_CONV, HY_COLS), SHORT_CONV ** -0.5),
        "hy_conv_b": nrm(ks[15], (DEPTH, HY_COLS), 0.01),
        "hy_w1": nrm(ks[16], (DEPTH, HY_EMB_DIM, HY_FILTER_ORDER), HY_EMB_DIM ** -0.5),
        "hy_b1": nrm(ks[17], (DEPTH, HY_FILTER_ORDER), 0.1),
        "hy_w2": nrm(ks[18], (DEPTH, HY_FILTER_ORDER, HY_FILTER_ORDER), HY_FILTER_ORDER ** -0.5),
        "hy_b2": nrm(ks[19], (DEPTH, HY_FILTER_ORDER), 0.1),
        "hy_w3": nrm(ks[20], (DEPTH, HY_FILTER_ORDER, HY_FILTER_ORDER), HY_FILTER_ORDER ** -0.5),
        "hy_b3": nrm(ks[21], (DEPTH, HY_FILTER_ORDER), 0.1),
        "hy_w4": nrm(ks[22], (DEPTH, HY_FILTER_ORDER, HY_ORDER * HY_DIRS * HY_WIDTH), 0.1 * HY_FILTER_ORDER ** -0.5),
        "hy_freq": gain(ks[23], (DEPTH, HY_FILTER_ORDER)),
        "hy_skip": nrm(ks[24], (DEPTH, HY_ORDER, HY_WIDTH), 0.5),
        "hy_norm": gain(ks[25], (DEPTH, HY_WIDTH)),
        "w_out": nrm(ks[26], (DEPTH, MIX_WIDTH, D_MODEL), MIX_WIDTH ** -0.5),
        "ffn_w_up": nrm(ks[27], (DEPTH, D_MODEL, 2 * D_FF), D_MODEL ** -0.5),
        "ffn_conv_w": nrm(ks[28], (DEPTH, SHORT_CONV, D_FF), SHORT_CONV ** -0.5),
        "ffn_conv_b": nrm(ks[29], (DEPTH, D_FF), 0.01),
        "ffn_w_down": nrm(ks[30], (DEPTH, D_FF, D_MODEL), D_FF ** -0.5),
        "final_norm": gain(ks[31], (D_MODEL,)),
    }


def reference(x, c, ctx, c_ctx, w_mod, b_mod, norm_mix, norm_ffn, w_in,
              lam_q1, lam_k1, lam_q2, lam_k2, subln,
              hy_conv_w, hy_conv_b, hy_w1, hy_b1, hy_w2, hy_b2, hy_w3, hy_b3, hy_w4,
              hy_freq, hy_skip, hy_norm, w_out,
              ffn_w_up, ffn_conv_w, ffn_conv_b, ffn_w_down, final_norm):
    x_lat, x_ctx = x, ctx
    silu_c = jax.nn.silu(c)
    silu_cc = jax.nn.silu(c_ctx)
    for l in range(DEPTH):
        p = {
            "norm_mix": norm_mix[l], "norm_ffn": norm_ffn[l], "w_in": w_in[l],
            "lam_q1": lam_q1[l], "lam_k1": lam_k1[l], "lam_q2": lam_q2[l], "lam_k2": lam_k2[l],
            "subln": subln[l], "hy_conv_w": hy_conv_w[l], "hy_conv_b": hy_conv_b[l],
            "hy_w1": hy_w1[l], "hy_b1": hy_b1[l], "hy_w2": hy_w2[l], "hy_b2": hy_b2[l],
            "hy_w3": hy_w3[l], "hy_b3": hy_b3[l], "hy_w4": hy_w4[l], "hy_freq": hy_freq[l],
            "hy_skip": hy_skip[l], "hy_norm": hy_norm[l], "w_out": w_out[l],
            "ffn_w_up": ffn_w_up[l], "ffn_conv_w": ffn_conv_w[l], "ffn_conv_b": ffn_conv_b[l],
            "ffn_w_down": ffn_w_down[l],
        }
        mod_x = silu_c @ w_mod[l] + b_mod[l]
        mod_c = silu_cc @ w_mod[l] + b_mod[l]
        x_lat, x_ctx = trunk_layer(x_lat, x_ctx, mod_x, mod_c, p, l, l < DEPTH - 1)
    return rmsnorm(x_lat, final_norm)
```

```python
import functools
import math

import jax
import jax.numpy as jnp
from jax import lax
from jax.experimental import pallas as pl
from jax.experimental.pallas import tpu as pltpu

F32 = jnp.float32
BF16 = jnp.bfloat16
HIGHEST = lax.Precision.HIGHEST

EPS = 1e-6
N_MOD = 6
GRID_W = 64
DIFF_HEADS = 4
DIFF_DH = 64
DIFF_VD = 128
ATTN_WIDTH = DIFF_HEADS * DIFF_VD
HY_ORDER = 2
HY_BANDS = 16
HY_DECAY_TARGET = 1e-2
HY_FAST_DECAY_PCT = 0.3
HY_SLOW_DECAY_PCT = 1.5
ROPE_BASE = 10000.0
LAM_INIT = 0.8 - 0.6 * math.exp(-0.3 * 0)

LANES = 128
SUBLANES = 8
Z_PITCH = LANES + SUBLANES
VMEM_LIMIT = 56 * 1024 * 1024
NEG_BIG = -1e30


def _cparams(sem):
    return pltpu.CompilerParams(dimension_semantics=sem, vmem_limit_bytes=VMEM_LIMIT)


def _rms(x, g):
    return x * lax.rsqrt(jnp.mean(x * x, axis=-1, keepdims=True) + EPS) * g


def _mod_kernel(c_ref, w_ref, b_ref, o_ref):
    c = c_ref[...]
    s = c * jax.nn.sigmoid(c)
    o_ref[...] = jnp.dot(s, w_ref[...], preferred_element_type=F32, precision=HIGHEST) + b_ref[...]


def _modulation(cvec, w_mod, b_mod):
    rows, d = cvec.shape
    cols = w_mod.shape[1]
    bc = 1024
    return pl.pallas_call(
        _mod_kernel,
        grid=(cols // bc,),
        in_specs=[pl.BlockSpec((rows, d), lambda j: (0, 0)),
                  pl.BlockSpec((d, bc), lambda j: (0, j)),
                  pl.BlockSpec((1, bc), lambda j: (0, j))],
        out_specs=pl.BlockSpec((rows, bc), lambda j: (0, j)),
        out_shape=jax.ShapeDtypeStruct((rows, cols), F32),
        compiler_params=_cparams(("arbitrary",)),
        name="mod",
    )(cvec, w_mod, b_mod.reshape(1, cols))


def _rope(t, cos, sin_signed, first_half):
    sw = jnp.where(first_half, pltpu.roll(t, LANES - 16, 1), pltpu.roll(t, 16, 1))
    return t * cos + sw * sin_signed


def _inproj_kernel(x_ref, mod_ref, g_ref, w_ref, cos_ref, sin_ref, q_ref, k_ref, v_ref, hy_ref):
    x = x_ref[0]
    m = mod_ref[0]
    h = (_rms(x, g_ref[...]) * (1.0 + m[1:2]) + m[0:1]).astype(BF16)
    cos = cos_ref[...]
    sin = sin_ref[...]
    lane = lax.broadcasted_iota(jnp.int32, cos.shape, 1)
    first_half = (lane % 32) < 16
    aw = ATTN_WIDTH
    for s in range(aw // LANES):
        c0 = s * LANES
        q = jnp.dot(h, w_ref[:, c0:c0 + LANES], preferred_element_type=F32)
        q_ref[0, :, c0:c0 + LANES] = (_rope(q, cos, sin, first_half) * (DIFF_DH ** -0.5)).astype(BF16)
        k = jnp.dot(h, w_ref[:, aw + c0:aw + c0 + LANES], preferred_element_type=F32)
        k_ref[0, :, c0:c0 + LANES] = _rope(k, cos, sin, first_half).astype(BF16)
    v_ref[0] = jnp.dot(h, w_ref[:, 2 * aw:3 * aw], preferred_element_type=F32).astype(BF16)
    hy_ref[0] = jnp.dot(h, w_ref[:, 3 * aw:], preferred_element_type=F32).astype(BF16)


def _inproj(x, modr, norm_g, w_in_bf, cos_t, sin_t):
    b, s, d = x.shape
    ts = min(512, s)
    aw = ATTN_WIDTH
    hyc = w_in_bf.shape[1] - 3 * aw
    return pl.pallas_call(
        _inproj_kernel,
        grid=(b, s // ts),
        in_specs=[pl.BlockSpec((1, ts, d), lambda i, j: (i, j, 0)),
                  pl.BlockSpec((1, 8, d), lambda i, j: (i, 0, 0)),
                  pl.BlockSpec((1, d), lambda i, j: (0, 0)),
                  pl.BlockSpec(w_in_bf.shape, lambda i, j: (0, 0)),
                  pl.BlockSpec((ts, LANES), lambda i, j: (j, 0)),
                  pl.BlockSpec((ts, LANES), lambda i, j: (j, 0))],
        out_specs=[pl.BlockSpec((1, ts, aw), lambda i, j: (i, j, 0)),
                   pl.BlockSpec((1, ts, aw), lambda i, j: (i, j, 0)),
                   pl.BlockSpec((1, ts, aw), lambda i, j: (i, j, 0)),
                   pl.BlockSpec((1, ts, hyc), lambda i, j: (i, j, 0))],
        out_shape=[jax.ShapeDtypeStruct((b, s, aw), BF16)] * 3 + [jax.ShapeDtypeStruct((b, s, hyc), BF16)],
        compiler_params=_cparams(("parallel", "parallel")),
        name="inproj",
    )(x, modr, norm_g, w_in_bf, cos_t, sin_t)


def _ctxproj_kernel(x_ref, mod_ref, g_ref, w_ref, k_ref, v_ref):
    m = mod_ref[0]
    h = (_rms(x_ref[0], g_ref[...]) * (1.0 + m[1:2]) + m[0:1]).astype(BF16)
    aw = ATTN_WIDTH
    k_ref[0] = jnp.dot(h, w_ref[:, :aw], preferred_element_type=F32).astype(BF16)
    v_ref[0] = jnp.dot(h, w_ref[:, aw:], preferred_element_type=F32).astype(BF16)


def _ctxproj(ctx, modr, norm_g, w_kv_bf):
    b, s, d = ctx.shape
    aw = ATTN_WIDTH
    nb = modr.shape[0] - 1
    return pl.pallas_call(
        _ctxproj_kernel,
        grid=(b,),
        in_specs=[pl.BlockSpec((1, s, d), lambda i: (i, 0, 0)),
                  pl.BlockSpec((1, 8, d), lambda i: (nb, 0, 0)),
                  pl.BlockSpec((1, d), lambda i: (0, 0)),
                  pl.BlockSpec(w_kv_bf.shape, lambda i: (0, 0))],
        out_specs=[pl.BlockSpec((1, s, aw), lambda i: (i, 0, 0))] * 2,
        out_shape=[jax.ShapeDtypeStruct((b, s, aw), BF16)] * 2,
        compiler_params=_cparams(("parallel",)),
        name="ctxproj",
    )(ctx, modr, norm_g, w_kv_bf)


def _attn_kernel(q_ref, k_ref, v_ref, lam_ref, sub_ref, o_ref, *, tk, nkv):
    q = q_ref[0]
    lane = lax.broadcasted_iota(jnp.int32, q.shape, 1)
    zero = jnp.zeros_like(q)
    qa = jnp.where(lane < DIFF_DH, q, zero)
    qb = jnp.where(lane >= DIFF_DH, q, zero)
    tq = q.shape[0]
    nt = (((1,), (1,)), ((), ()))

    def step(qm, kj, vj, m, l, acc):
        s = lax.dot_general(qm, kj, nt, preferred_element_type=F32)
        mx = jnp.maximum(m, jnp.max(s, axis=-1, keepdims=True))
        alpha = jnp.exp(m - mx)
        p = jnp.exp(s - mx)
        l = alpha * l + jnp.sum(p, axis=-1, keepdims=True)
        acc = alpha * acc + jnp.dot(p.astype(BF16), vj, preferred_element_type=F32)
        return mx, l, acc

    def body(j, carry):
        m0, l0, a0, m1, l1, a1 = carry
        r0 = pl.multiple_of(j * tk, tk)
        kj = k_ref[0, pl.ds(r0, tk), :]
        vj = v_ref[0, pl.ds(r0, tk), :]
        m0, l0, a0 = step(qa, kj, vj, m0, l0, a0)
        m1, l1, a1 = step(qb, kj, vj, m1, l1, a1)
        return m0, l0, a0, m1, l1, a1

    mi = jnp.full((tq, 1), NEG_BIG, F32)
    li = jnp.zeros((tq, 1), F32)
    ai = jnp.zeros((tq, DIFF_VD), F32)
    m0, l0, a0, m1, l1, a1 = lax.fori_loop(0, nkv, body, (mi, li, ai, mi, li, ai))
    lp = lam_ref[...]
    lam = (jnp.exp(jnp.sum(lp[0:1] * lp[1:2], axis=-1, keepdims=True))
           - jnp.exp(jnp.sum(lp[2:3] * lp[3:4], axis=-1, keepdims=True)) + LAM_INIT)
    o = a0 / l0 - lam * (a1 / l1)
    o_ref[0] = (_rms(o, sub_ref[...]) * (1.0 - LAM_INIT)).astype(BF16)


def _pick_tile(n, cands):
    for c in cands:
        if n % c == 0:
            return c
    raise ValueError(f"no tile for {n}")


def _attention(q, k_all, v_all, lamp, subln):
    b, s, aw = q.shape
    sk = k_all.shape[1]
    tq = min(512, s)
    tk = _pick_tile(sk, (768, 512, 384, 256, 128))
    kern = functools.partial(_attn_kernel, tk=tk, nkv=sk // tk)
    return pl.pallas_call(
        kern,
        grid=(b, DIFF_HEADS, s // tq),
        in_specs=[pl.BlockSpec((1, tq, LANES), lambda i, h, j: (i, j, h)),
                  pl.BlockSpec((1, sk, LANES), lambda i, h, j: (i, 0, h)),
                  pl.BlockSpec((1, sk, LANES), lambda i, h, j: (i, 0, h)),
                  pl.BlockSpec(lamp.shape, lambda i, h, j: (0, 0)),
                  pl.BlockSpec((1, DIFF_VD), lambda i, h, j: (0, 0))],
        out_specs=pl.BlockSpec((1, tq, LANES), lambda i, h, j: (i, j, h)),
        out_shape=jax.ShapeDtypeStruct((b, s, aw), BF16),
        compiler_params=_cparams(("parallel", "parallel", "parallel")),
        name="attn",
    )(q, k_all, v_all, lamp, subln)


def _filt_kernel(f_ref, w1_ref, b1_ref, w2_ref, b2_ref, w3_ref, b3_ref, fr_ref, w4_ref, dl_ref, o_ref):
    f = f_ref[...]
    fr = fr_ref[...]
    dot = functools.partial(jnp.dot, preferred_element_type=F32, precision=HIGHEST)
    h = jnp.sin(fr * (dot(f, w1_ref[...]) + b1_ref[...]))
    h = jnp.sin(fr * (dot(h, w2_ref[...]) + b2_ref[...]))
    h = jnp.sin(fr * (dot(h, w3_ref[...]) + b3_ref[...]))
    t = f[:, 0:1]
    valid = f[:, 2 * HY_BANDS + 1:2 * HY_BANDS + 2]
    o_ref[...] = dot(h, w4_ref[0]) * jnp.exp(-t * dl_ref[0]) * valid


def _hyena_filters(feat, p, w4r, deltas):
    n = feat.shape[0]
    tr = min(1024, n // 2)
    half_steps = (n // 2) // tr
    fo = p["hy_w2"].shape[0]
    cols = w4r.shape[2]
    const = lambda shape: pl.BlockSpec(shape, lambda i: (0,) * len(shape))
    w1 = jnp.zeros((LANES, fo), F32).at[:p["hy_w1"].shape[0]].set(p["hy_w1"])
    return pl.pallas_call(
        _filt_kernel,
        grid=(n // tr,),
        in_specs=[pl.BlockSpec((tr, LANES), lambda i: (i, 0)),
                  const((LANES, fo)), const((1, fo)), const((fo, fo)), const((1, fo)),
                  const((fo, fo)), const((1, fo)), const((1, fo)),
                  pl.BlockSpec((1, fo, cols), lambda i: (i // half_steps, 0, 0)),
                  pl.BlockSpec((1, 1, cols), lambda i: (i // half_steps, 0, 0))],
        out_specs=pl.BlockSpec((tr, cols), lambda i: (i, 0)),
        out_shape=jax.ShapeDtypeStruct((n, cols), F32),
        compiler_params=_cparams(("parallel",)),
        name="filt",
    )(feat, w1, p["hy_b1"].reshape(1, fo), p["hy_w2"], p["hy_b2"].reshape(1, fo),
      p["hy_w3"], p["hy_b3"].reshape(1, fo), p["hy_freq"].reshape(1, fo), w4r, deltas)


def _dft_tables(n1):
    n2 = LANES
    n = n1 * n2
    h1 = n1 // 2
    i2 = jnp.arange(n2, dtype=jnp.int32)
    i1 = jnp.arange(n1, dtype=jnp.int32)
    idx = (i2[:, None, None] * i1[None, :, None] + n2 * i1[None, :, None] * i1[None, None, :]) % n
    ang = idx.astype(F32) * (-2.0 * math.pi / n)
    gr, gi = jnp.cos(ang), jnp.sin(ang)
    grh, gih = gr[:, :, :h1], gi[:, :, :h1]
    g1_sig = jnp.concatenate([jnp.concatenate([grh, -gih], 2), jnp.concatenate([gih, grh], 2)], 1)
    g1_flt = jnp.concatenate([gr, gi], 1)
    qr, qi = jnp.swapaxes(grh, 1, 2), -jnp.swapaxes(gih, 1, 2)
    g3 = jnp.concatenate([jnp.concatenate([qr, -qi], 2), jnp.concatenate([qi, qr], 2)], 1)
    a2 = ((i2[:, None] * i2[None, :]) % n2).astype(F32) * (-2.0 * math.pi / n2)
    fr, fi = jnp.cos(a2), jnp.sin(a2)
    f2 = jnp.concatenate([jnp.concatenate([fr, -fi], 1), jnp.concatenate([fi, fr], 1)], 0)
    f2i = jnp.concatenate([jnp.concatenate([fr, fi], 1), jnp.concatenate([-fi, fr], 1)], 0)
    return (g1_sig.astype(BF16), g1_flt.astype(BF16), g3.astype(BF16), f2.astype(BF16), f2i.astype(BF16))


def _a_pitch(n1):
    return 2 * n1 + SUBLANES


def _stage1(zy_ref, a_ref, g_ref, step, it, n1):
    pa = _a_pitch(n1)

    def body(i, c):
        n2 = step * it + i
        rhs = zy_ref[pl.ds(n2, n1, stride=Z_PITCH), :].astype(BF16)
        a_ref[pl.ds(pl.multiple_of(n2 * pa, SUBLANES), 2 * n1), :] = jnp.dot(
            g_ref[i], rhs, preferred_element_type=F32)
        return c

    lax.fori_loop(0, it, body, 0)


def _gather_k1(a_ref, k1, n1):
    pa = _a_pitch(n1)
    re = a_ref[pl.ds(k1, LANES, stride=pa), :]
    im = a_ref[pl.ds(n1 + k1, LANES, stride=pa), :]
    return jnp.concatenate([re, im], axis=0).astype(BF16)


def _load_blocks(dst_ref, src, nblk, base):
    def body(i, c):
        r0 = pl.multiple_of(i * LANES, LANES)
        d0 = pl.multiple_of((base + i) * Z_PITCH, SUBLANES)
        dst_ref[pl.ds(d0, LANES), :] = src(r0).astype(F32)
        return c

    lax.fori_loop(0, nblk, body, 0)


def _fdft_kernel(k_ref, g1_ref, f2_ref, o_ref, zy_ref, a_ref, *, n1, it1, itm, s1):
    t = pl.program_id(1)

    @pl.when(t == 0)
    def _():
        _load_blocks(zy_ref, lambda r0: k_ref[pl.ds(r0, LANES), :], n1, 0)

    @pl.when(t < s1)
    def _():
        _stage1(zy_ref, a_ref, g1_ref, t, it1, n1)

    @pl.when(t >= s1)
    def _():
        f2 = f2_ref[...]
        scale = 1.0 / (n1 * LANES)

        def body(i, c):
            k1 = (t - s1) * itm + i
            y = jnp.dot(f2, _gather_k1(a_ref, k1, n1), preferred_element_type=F32)
            o_ref[0, i] = (y * scale).astype(BF16)
            return c

        lax.fori_loop(0, itm, body, 0)


def _filter_spectra(kfull, g1_flt, f2, n1):
    n, cols = kfull.shape
    nslab = cols // LANES
    it1 = 16
    itm = min(16, n1)
    s1 = LANES // it1
    sm = n1 // itm
    kern = functools.partial(_fdft_kernel, n1=n1, it1=it1, itm=itm, s1=s1)
    return pl.pallas_call(
        kern,
        grid=(nslab, s1 + sm),
        in_specs=[pl.BlockSpec((n, LANES), lambda s, t: (0, s)),
                  pl.BlockSpec((it1, 2 * n1, n1), lambda s, t: (jnp.minimum(t, s1 - 1), 0, 0)),
                  pl.BlockSpec((2 * LANES, 2 * LANES), lambda s, t: (0, 0))],
        out_specs=pl.BlockSpec((1, itm, 2 * LANES, LANES), lambda s, t: (s, jnp.maximum(t - s1, 0), 0, 0)),
        out_shape=jax.ShapeDtypeStruct((nslab, n1, 2 * LANES, LANES), BF16),
        scratch_shapes=[pltpu.VMEM((n1 * Z_PITCH, LANES), F32),
                        pltpu.VMEM((LANES * _a_pitch(n1), LANES), F32)],
        compiler_params=_cparams(("parallel", "arbitrary")),
        name="fdft",
    )(kfull, g1_flt, f2)


def _hconv_kernel(x_ref, w_ref, b_ref, o_ref):
    x = x_ref[0].astype(F32)
    s = x.shape[0]
    row = lax.broadcasted_iota(jnp.int32, x.shape, 0)
    prev = jnp.where(row == 0, 0.0, pltpu.roll(x, 1, 0))
    nxt = jnp.where(row == s - 1, 0.0, pltpu.roll(x, s - 1, 0))
    w = w_ref[...]
    o_ref[0] = (prev * w[0:1] + x * w[1:2] + nxt * w[2:3] + b_ref[...]).astype(BF16)


def _hyena_short_conv(hy, w, bias):
    b, s, c = hy.shape
    return pl.pallas_call(
        _hconv_kernel,
        grid=(b, c // LANES),
        in_specs=[pl.BlockSpec((1, s, LANES), lambda i, j: (i, 0, j)),
                  pl.BlockSpec((3, LANES), lambda i, j: (0, j)),
                  pl.BlockSpec((1, LANES), lambda i, j: (0, j))],
        out_specs=pl.BlockSpec((1, s, LANES), lambda i, j: (i, 0, j)),
        out_shape=jax.ShapeDtypeStruct((b, s, c), BF16),
        compiler_params=_cparams(("parallel", "parallel")),
        name="hconv",
    )(hy, w, bias.reshape(1, c))


def _hyena_kernel(z_ref, gate_ref, skip_ref, g1_ref, kf_ref, g3_ref, f2_ref, f2i_ref, o_ref,
                  zy_ref, a_ref, *, n1, it1, itm, s1, sm):
    t = pl.program_id(1)
    h1 = n1 // 2
    pa = _a_pitch(n1)

    @pl.when(t == 0)
    def _():
        for bb in range(2):
            _load_blocks(zy_ref, lambda r0, bb=bb: z_ref[bb, pl.ds(r0, LANES), :], h1, bb * h1)

    @pl.when(t < s1)
    def _():
        _stage1(zy_ref, a_ref, g1_ref, t, it1, n1)

    @pl.when(jnp.logical_and(t >= s1, t < s1 + sm))
    def _():
        f2 = f2_ref[...]
        f2i = f2i_ref[...]

        def body(i, c):
            k1 = (t - s1) * itm + i
            y = jnp.dot(f2, _gather_k1(a_ref, k1, n1), preferred_element_type=F32)
            kf = kf_ref[0, i].astype(F32)
            yr, yi = y[:LANES], y[LANES:]
            kr, ki = kf[:LANES], kf[LANES:]
            prod = jnp.concatenate([yr * kr - yi * ki, yr * ki + yi * kr], axis=0).astype(BF16)
            r = jnp.dot(f2i, prod, preferred_element_type=F32)
            a_ref[pl.ds(k1, LANES, stride=pa), :] = r[:LANES]
            a_ref[pl.ds(n1 + k1, LANES, stride=pa), :] = r[LANES:]
            return c

        lax.fori_loop(0, itm, body, 0)

    @pl.when(t >= s1 + sm)
    def _():
        def body(i, c):
            n2 = (t - s1 - sm) * it1 + i
            rhs = a_ref[pl.ds(pl.multiple_of(n2 * pa, SUBLANES), 2 * n1), :].astype(BF16)
            zy_ref[pl.ds(n2, n1, stride=Z_PITCH), :] = jnp.dot(g3_ref[i], rhs, preferred_element_type=F32)
            return c

        lax.fori_loop(0, it1, body, 0)

    @pl.when(t == s1 + sm + s1 - 1)
    def _():
        skip = skip_ref[...]
        for bb in range(2):
            def body(i, c, bb=bb):
                r0 = pl.multiple_of(i * LANES, LANES)
                d0 = pl.multiple_of((bb * h1 + i) * Z_PITCH, SUBLANES)
                y = zy_ref[pl.ds(d0, LANES), :]
                zz = z_ref[bb, pl.ds(r0, LANES), :].astype(F32)
                gg = gate_ref[bb, pl.ds(r0, LANES), :].astype(F32)
                o_ref[bb, pl.ds(r0, LANES), :] = (gg * (y + skip * zz)).astype(BF16)
                return c

            lax.fori_loop(0, h1, body, 0)


def _hyena_order(z, z_off, gate, gate_off, skip, kf, kf_off, tables, n1):
    g1_sig, _, g3, f2, f2i = tables
    b, s, _ = z.shape
    width = skip.shape[1]
    nslab = width // LANES
    it1 = 16
    itm = min(16, n1)
    s1 = LANES // it1
    sm = n1 // itm
    steps = 2 * s1 + sm
    kern = functools.partial(_hyena_kernel, n1=n1, it1=it1, itm=itm, s1=s1, sm=sm)
    one = pl.Buffered(1)
    return pl.pallas_call(
        kern,
        grid=((b // 2) * nslab, steps),
        in_specs=[pl.BlockSpec((2, s, LANES), lambda g, t: (g // nslab, 0, z_off + g % nslab), pipeline_mode=one),
                  pl.BlockSpec((2, s, LANES), lambda g, t: (g // nslab, 0, gate_off + g % nslab),
                               pipeline_mode=one),
                  pl.BlockSpec((1, LANES), lambda g, t: (0, g % nslab)),
                  pl.BlockSpec((it1, 2 * n1, n1), lambda g, t: (jnp.minimum(t, s1 - 1), 0, 0)),
                  pl.BlockSpec((1, itm, 2 * LANES, LANES),
                               lambda g, t: (kf_off + g % nslab, jnp.clip(t - s1, 0, sm - 1), 0, 0)),
                  pl.BlockSpec((it1, n1, 2 * n1), lambda g, t: (jnp.clip(t - s1 - sm, 0, s1 - 1), 0, 0)),
                  pl.BlockSpec((2 * LANES, 2 * LANES), lambda g, t: (0, 0)),
                  pl.BlockSpec((2 * LANES, 2 * LANES), lambda g, t: (0, 0))],
        out_specs=pl.BlockSpec((2, s, LANES), lambda g, t: (g // nslab, 0, g % nslab)),
        out_shape=jax.ShapeDtypeStruct((b, s, width), BF16),
        scratch_shapes=[pltpu.VMEM((n1 * Z_PITCH, LANES), F32),
                        pltpu.VMEM((LANES * _a_pitch(n1), LANES), F32)],
        compiler_params=_cparams(("parallel", "arbitrary")),
        name="hyena",
    )(z, gate, skip, g1_sig, kf, g3, f2, f2i)


def _outproj_kernel(a_ref, z_ref, x_ref, mod_ref, hn_ref, w_ref, o_ref):
    aw = ATTN_WIDTH
    zn = _rms(z_ref[0].astype(F32), hn_ref[...]).astype(BF16)
    y = (jnp.dot(a_ref[0], w_ref[:aw, :], preferred_element_type=F32)
         + jnp.dot(zn, w_ref[aw:, :], preferred_element_type=F32))
    o_ref[0] = x_ref[0] + mod_ref[0][2:3] * y


def _outproj(attn, hz, x, modr, hy_norm, w_out_bf):
    b, s, d = x.shape
    ts = min(512, s)
    aw = attn.shape[2]
    hw = hz.shape[2]
    return pl.pallas_call(
        _outproj_kernel,
        grid=(b, s // ts),
        in_specs=[pl.BlockSpec((1, ts, aw), lambda i, j: (i, j, 0)),
                  pl.BlockSpec((1, ts, hw), lambda i, j: (i, j, 0)),
                  pl.BlockSpec((1, ts, d), lambda i, j: (i, j, 0)),
                  pl.BlockSpec((1, 8, d), lambda i, j: (i, 0, 0)),
                  pl.BlockSpec((1, hw), lambda i, j: (0, 0)),
                  pl.BlockSpec(w_out_bf.shape, lambda i, j: (0, 0))],
        out_specs=pl.BlockSpec((1, ts, d), lambda i, j: (i, j, 0)),
        out_shape=jax.ShapeDtypeStruct((b, s, d), F32),
        compiler_params=_cparams(("parallel", "parallel")),
        name="outproj",
    )(attn, hz, x, modr, hy_norm, w_out_bf)


def _ffn_kernel(x_ref, xp_ref, xn_ref, mod_ref, g_ref, wu_ref, cw_ref, cb_ref, wd_ref, fg_ref, o_ref,
                *, dff, cf):
    j = pl.program_id(1)
    last = pl.num_programs(1) - 1
    xm = x_ref[0]
    ts = xm.shape[0]
    m = mod_ref[0]
    xa = jnp.concatenate([xp_ref[0], xm, xn_ref[0]], axis=0)
    h = (_rms(xa, g_ref[...]) * (1.0 + m[4:5]) + m[3:4]).astype(BF16)
    hm = h[SUBLANES:SUBLANES + ts]
    rows = ts + 2 * SUBLANES
    row = lax.broadcasted_iota(jnp.int32, (ts, 1), 0)
    no_prev = jnp.logical_and(j == 0, row == 0)
    no_next = jnp.logical_and(j == last, row == ts - 1)
    acc = jnp.zeros((ts, xm.shape[1]), F32)
    for c in range(dff // cf):
        c0 = c * cf
        a = jnp.dot(h, wu_ref[:, c0:c0 + cf], preferred_element_type=F32)
        gate = jnp.dot(hm, wu_ref[:, dff + c0:dff + c0 + cf], preferred_element_type=F32)
        prev = jnp.where(no_prev, 0.0, pltpu.roll(a, 1, 0)[SUBLANES:SUBLANES + ts])
        nxt = jnp.where(no_next, 0.0, pltpu.roll(a, rows - 1, 0)[SUBLANES:SUBLANES + ts])
        cw = cw_ref[:, c0:c0 + cf]
        ac = prev * cw[0:1] + a[SUBLANES:SUBLANES + ts] * cw[1:2] + nxt * cw[2:3] + cb_ref[:, c0:c0 + cf]
        gl = 0.5 * ac * (1.0 + lax.erf(ac * (2.0 ** -0.5)))
        acc = acc + jnp.dot((gl * gate).astype(BF16), wd_ref[c0:c0 + cf, :], preferred_element_type=F32)
    y = xm + m[5:6] * acc
    o_ref[0] = _rms(y, fg_ref[...])


def _ffn(x1, modr, norm_g, wu_bf, conv_w, conv_b, wd_bf, final_g):
    b, s, d = x1.shape
    dff = wd_bf.shape[0]
    ts = min(512, s)
    nh = ts // SUBLANES
    nblk8 = s // SUBLANES
    cf = dff // 2 if (dff // 2) % LANES == 0 else dff
    kern = functools.partial(_ffn_kernel, dff=dff, cf=cf)
    one = pl.Buffered(1)
    return pl.pallas_call(
        kern,
        grid=(b, s // ts),
        in_specs=[pl.BlockSpec((1, ts, d), lambda i, j: (i, j, 0)),
                  pl.BlockSpec((1, SUBLANES, d), lambda i, j: (i, jnp.maximum(j * nh - 1, 0), 0)),
                  pl.BlockSpec((1, SUBLANES, d), lambda i, j: (i, jnp.minimum((j + 1) * nh, nblk8 - 1), 0)),
                  pl.BlockSpec((1, 8, d), lambda i, j: (i, 0, 0)),
                  pl.BlockSpec((1, d), lambda i, j: (0, 0)),
                  pl.BlockSpec(wu_bf.shape, lambda i, j: (0, 0), pipeline_mode=one),
                  pl.BlockSpec(conv_w.shape, lambda i, j: (0, 0)),
                  pl.BlockSpec((1, dff), lambda i, j: (0, 0)),
                  pl.BlockSpec(wd_bf.shape, lambda i, j: (0, 0), pipeline_mode=one),
                  pl.BlockSpec((1, d), lambda i, j: (0, 0))],
        out_specs=pl.BlockSpec((1, ts, d), lambda i, j: (i, j, 0)),
        out_shape=jax.ShapeDtypeStruct((b, s, d), F32),
        compiler_params=_cparams(("parallel", "parallel")),
        name="ffn",
    )(x1, x1, x1, modr, norm_g, wu_bf, conv_w, conv_b.reshape(1, dff), wd_bf, final_g)


def _rope_tables(s):
    nf = DIFF_DH // 4
    inv = ROPE_BASE ** (-jnp.arange(nf, dtype=F32) / nf)
    t = jnp.arange(s, dtype=jnp.int32)
    row = (t // GRID_W).astype(F32)[:, None] * inv
    col = (t % GRID_W).astype(F32)[:, None] * inv
    cos = jnp.concatenate([jnp.cos(row), jnp.cos(row), jnp.cos(col), jnp.cos(col)], axis=1)
    sin = jnp.concatenate([-jnp.sin(row), jnp.sin(row), -jnp.sin(col), jnp.sin(col)], axis=1)
    return jnp.tile(cos, (1, LANES // DIFF_DH)), jnp.tile(sin, (1, LANES // DIFF_DH))


def _filter_features(l):
    pos = jnp.concatenate([jnp.arange(l, dtype=jnp.int32), l - jnp.arange(l, dtype=jnp.int32)])
    valid = jnp.ones((2 * l,), F32).at[l].set(0.0)
    pos = jnp.where(pos == l, 0, pos)
    tt = jnp.linspace(0.0, 1.0, l, dtype=F32)[pos][:, None]
    w = (2.0 * math.pi * jnp.arange(l, dtype=F32) / l)[pos][:, None]
    f = jnp.linspace(1e-4, HY_BANDS - 1, HY_BANDS, dtype=F32)[None, :]
    feat = jnp.concatenate([tt, jnp.cos(f * w), -jnp.sin(f * w), valid[:, None]], axis=-1)
    return jnp.pad(feat, ((0, 0), (0, LANES - feat.shape[1])))


def kernel(x, c, ctx, c_ctx, w_mod, b_mod, norm_mix, norm_ffn, w_in, lam_q1, lam_k1, lam_q2, lam_k2, subln, hy_conv_w, hy_conv_b, hy_w1, hy_b1, hy_w2, hy_b2, hy_w3, hy_b3, hy_w4, hy_freq, hy_skip, hy_norm, w_out, ffn_w_up, ffn_conv_w, ffn_conv_b, ffn_w_down, final_norm):
    b, s, d = x.shape
    assert w_mod.shape[0] == 1 and b % 2 == 0 and (2 * s) % (LANES * 16) == 0
    aw = ATTN_WIDTH
    hw = hy_skip.shape[2]
    n1 = 2 * s // LANES

    rows = ((b + 1 + 7) // 8) * 8
    cvec = jnp.zeros((rows, d), F32).at[:b].set(c).at[b].set(c_ctx)
    mod = _modulation(cvec, w_mod[0], b_mod[0])[:b + 1]
    modr = jnp.pad(mod.reshape(b + 1, N_MOD, d), ((0, 0), (0, 8 - N_MOD), (0, 0)))

    w_in_bf = w_in[0].astype(BF16)
    cos_t, sin_t = _rope_tables(s)
    g_mix = norm_mix[0].reshape(1, d)
    q, k_x, v_x, hy = _inproj(x, modr, g_mix, w_in_bf, cos_t, sin_t)
    k_c, v_c = _ctxproj(ctx, modr, g_mix, w_in_bf[:, aw:3 * aw])
    k_all = jnp.concatenate([k_c, k_x], axis=1)
    v_all = jnp.concatenate([v_c, v_x], axis=1)
    lamp = jnp.concatenate([lam_q1, lam_k1, lam_q2, lam_k2], axis=0)
    attn = _attention(q, k_all, v_all, lamp, subln[0].reshape(1, DIFF_VD))

    p = {"hy_w1": hy_w1[0], "hy_b1": hy_b1[0], "hy_w2": hy_w2[0], "hy_b2": hy_b2[0],
         "hy_w3": hy_w3[0], "hy_b3": hy_b3[0], "hy_freq": hy_freq[0]}
    fo = hy_w4.shape[1]
    w4r = hy_w4[0].reshape(fo, HY_ORDER, 2, hw).transpose(2, 0, 1, 3).reshape(2, fo, HY_ORDER * hw)
    min_decay = math.log(HY_DECAY_TARGET) / HY_FAST_DECAY_PCT
    max_decay = math.log(HY_DECAY_TARGET) / HY_SLOW_DECAY_PCT
    deltas = jnp.abs(jnp.linspace(min_decay, max_decay, HY_ORDER * 2 * hw, dtype=F32))
    deltas = deltas.reshape(HY_ORDER, 2, hw).transpose(1, 0, 2).reshape(2, 1, HY_ORDER * hw)
    kfull = _hyena_filters(_filter_features(s), p, w4r, deltas)
    tables = _dft_tables(n1)
    kf = _filter_spectra(kfull, tables[1], tables[3], n1)

    u = _hyena_short_conv(hy, hy_conv_w[0], hy_conv_b[0])
    nslab = hw // LANES
    z = _hyena_order(u, 0, u, nslab, hy_skip[0, 0:1], kf, 0, tables, n1)
    z = _hyena_order(z, 0, u, 2 * nslab, hy_skip[0, 1:2], kf, nslab, tables, n1)

    x1 = _outproj(attn, z, x, modr, hy_norm[0].reshape(1, hw), w_out[0].astype(BF16))
    return _ffn(x1, modr, norm_ffn[0].reshape(1, d), ffn_w_up[0].astype(BF16), ffn_conv_w[0],
                ffn_conv_b[0], ffn_w_down[0].astype(BF16), final_norm.reshape(1, d))
```

```python
import functools
import math

import jax
import jax.numpy as jnp
from jax import lax
from jax.experimental import pallas as pl
from jax.experimental.pallas import tpu as pltpu

F32 = jnp.float32
BF16 = jnp.bfloat16
HIGHEST = lax.Precision.HIGHEST

EPS = 1e-6
N_MOD = 6
GRID_W = 64
DIFF_HEADS = 4
DIFF_DH = 64
DIFF_VD = 128
ATTN_WIDTH = DIFF_HEADS * DIFF_VD
HY_ORDER = 2
HY_BANDS = 16
HY_DECAY_TARGET = 1e-2
HY_FAST_DECAY_PCT = 0.3
HY_SLOW_DECAY_PCT = 1.5
ROPE_BASE = 10000.0
LAM_INIT = 0.8 - 0.6 * math.exp(-0.3 * 0)

LANES = 128
SUBLANES = 8
Z_PITCH = LANES + SUBLANES
DFT_UNROLL = 8
VMEM_LIMIT = 56 * 1024 * 1024
NEG_BIG = -1e30
Q_SCALE = DIFF_DH ** -0.5 * math.log2(math.e)


def _cparams(sem):
    return pltpu.CompilerParams(dimension_semantics=sem, vmem_limit_bytes=VMEM_LIMIT)


def _rms(x, g):
    return x * lax.rsqrt(jnp.mean(x * x, axis=-1, keepdims=True) + EPS) * g


def _mod_kernel(c_ref, w_ref, b_ref, o_ref):
    c = c_ref[...]
    s = c * jax.nn.sigmoid(c)
    o_ref[...] = jnp.dot(s, w_ref[...], preferred_element_type=F32, precision=HIGHEST) + b_ref[...]


def _modulation(cvec, w_mod, b_mod):
    rows, d = cvec.shape
    cols = w_mod.shape[1]
    bc = 1024
    return pl.pallas_call(
        _mod_kernel,
        grid=(cols // bc,),
        in_specs=[pl.BlockSpec((rows, d), lambda j: (0, 0)),
                  pl.BlockSpec((d, bc), lambda j: (0, j)),
                  pl.BlockSpec((1, bc), lambda j: (0, j))],
        out_specs=pl.BlockSpec((rows, bc), lambda j: (0, j)),
        out_shape=jax.ShapeDtypeStruct((rows, cols), F32),
        compiler_params=_cparams(("arbitrary",)),
        name="mod",
    )(cvec, w_mod, b_mod.reshape(1, cols))


def _rope(t, cos, sin_signed, first_half):
    sw = jnp.where(first_half, pltpu.roll(t, LANES - 16, 1), pltpu.roll(t, 16, 1))
    return t * cos + sw * sin_signed


def _inproj_kernel(x_ref, mod_ref, g_ref, w_ref, cos_ref, sin_ref, q_ref, k_ref, v_ref, hy_ref):
    x = x_ref[0]
    m = mod_ref[0]
    h = (_rms(x, g_ref[...]) * (1.0 + m[1:2]) + m[0:1]).astype(BF16)
    cos = cos_ref[...]
    sin = sin_ref[...]
    lane = lax.broadcasted_iota(jnp.int32, cos.shape, 1)
    first_half = (lane % 32) < 16
    aw = ATTN_WIDTH
    for s in range(aw // LANES):
        c0 = s * LANES
        q = jnp.dot(h, w_ref[:, c0:c0 + LANES], preferred_element_type=F32)
        q_ref[0, :, c0:c0 + LANES] = (_rope(q, cos, sin, first_half) * Q_SCALE).astype(BF16)
        k = jnp.dot(h, w_ref[:, aw + c0:aw + c0 + LANES], preferred_element_type=F32)
        k_ref[0, :, c0:c0 + LANES] = _rope(k, cos, sin, first_half).astype(BF16)
    v_ref[0] = jnp.dot(h, w_ref[:, 2 * aw:3 * aw], preferred_element_type=F32).astype(BF16)
    hy_ref[0] = jnp.dot(h, w_ref[:, 3 * aw:], preferred_element_type=F32).astype(BF16)


def _inproj(x, modr, norm_g, w_in_bf, cos_t, sin_t):
    b, s, d = x.shape
    ts = min(512, s)
    aw = ATTN_WIDTH
    hyc = w_in_bf.shape[1] - 3 * aw
    return pl.pallas_call(
        _inproj_kernel,
        grid=(b, s // ts),
        in_specs=[pl.BlockSpec((1, ts, d), lambda i, j: (i, j, 0)),
                  pl.BlockSpec((1, 8, d), lambda i, j: (i, 0, 0)),
                  pl.BlockSpec((1, d), lambda i, j: (0, 0)),
                  pl.BlockSpec(w_in_bf.shape, lambda i, j: (0, 0)),
                  pl.BlockSpec((ts, LANES), lambda i, j: (j, 0)),
                  pl.BlockSpec((ts, LANES), lambda i, j: (j, 0))],
        out_specs=[pl.BlockSpec((1, ts, aw), lambda i, j: (i, j, 0)),
                   pl.BlockSpec((1, ts, aw), lambda i, j: (i, j, 0)),
                   pl.BlockSpec((1, ts, aw), lambda i, j: (i, j, 0)),
                   pl.BlockSpec((1, ts, hyc), lambda i, j: (i, j, 0))],
        out_shape=[jax.ShapeDtypeStruct((b, s, aw), BF16)] * 3 + [jax.ShapeDtypeStruct((b, s, hyc), BF16)],
        compiler_params=_cparams(("parallel", "parallel")),
        name="inproj",
    )(x, modr, norm_g, w_in_bf, cos_t, sin_t)


def _ctxproj_kernel(x_ref, mod_ref, g_ref, w_ref, k_ref, v_ref):
    m = mod_ref[0]
    h = (_rms(x_ref[0], g_ref[...]) * (1.0 + m[1:2]) + m[0:1]).astype(BF16)
    aw = ATTN_WIDTH
    k_ref[0] = jnp.dot(h, w_ref[:, :aw], preferred_element_type=F32).astype(BF16)
    v_ref[0] = jnp.dot(h, w_ref[:, aw:], preferred_element_type=F32).astype(BF16)


def _ctxproj(ctx, modr, norm_g, w_kv_bf):
    b, s, d = ctx.shape
    aw = ATTN_WIDTH
    nb = modr.shape[0] - 1
    return pl.pallas_call(
        _ctxproj_kernel,
        grid=(b,),
        in_specs=[pl.BlockSpec((1, s, d), lambda i: (i, 0, 0)),
                  pl.BlockSpec((1, 8, d), lambda i: (nb, 0, 0)),
                  pl.BlockSpec((1, d), lambda i: (0, 0)),
                  pl.BlockSpec(w_kv_bf.shape, lambda i: (0, 0))],
        out_specs=[pl.BlockSpec((1, s, aw), lambda i: (i, 0, 0))] * 2,
        out_shape=[jax.ShapeDtypeStruct((b, s, aw), BF16)] * 2,
        compiler_params=_cparams(("parallel",)),
        name="ctxproj",
    )(ctx, modr, norm_g, w_kv_bf)


def _attn_kernel(q_ref, k_ref, v_ref, lam_ref, sub_ref, o_ref, qm_scr, sa_scr, sb_scr, acc_scr, *, tk, nkv):
    q = q_ref[0]
    lane = lax.broadcasted_iota(jnp.int32, q.shape, 1)
    zero = jnp.zeros_like(q)
    qm_scr[0] = jnp.where(lane < DIFF_DH, q, zero)
    qm_scr[1] = jnp.where(lane >= DIFF_DH, q, zero)
    acc_scr[...] = jnp.zeros(acc_scr.shape, F32)
    tq = q.shape[0]
    nt = (((1,), (1,)), ((), ()))
    ones = jnp.ones((tk, DIFF_VD), BF16)

    def scores(j, s_scr, m_cur):
        kj = k_ref[0, pl.ds(pl.multiple_of(j * tk, tk), tk), :]
        m_next = []
        for mp in range(2):
            s = lax.dot_general(qm_scr[mp], kj, nt, preferred_element_type=F32)
            s_scr[mp] = s
            m_next.append(jnp.maximum(m_cur[mp], jnp.max(s, axis=-1, keepdims=True)))
        return m_next

    def accumulate(j, s_scr, m_prev, m_cur):
        vj = v_ref[0, pl.ds(pl.multiple_of(j * tk, tk), tk), :]
        vext = jnp.concatenate([vj, ones], axis=1)
        for mp in range(2):
            p = jnp.exp2(s_scr[mp] - m_cur[mp]).astype(BF16)
            alpha = jnp.exp2(m_prev[mp] - m_cur[mp])
            acc_scr[mp] = alpha * acc_scr[mp] + jnp.dot(p, vext, preferred_element_type=F32)

    m_init = [jnp.full((tq, 1), NEG_BIG, F32)] * 2
    m_first = scores(0, sa_scr, m_init)

    def pair(i, carry):
        m_prev, m_cur = list(carry[:2]), list(carry[2:])
        j = 2 * i
        m_nxt = scores(j + 1, sb_scr, m_cur)
        accumulate(j, sa_scr, m_prev, m_cur)
        m_nn = scores(j + 2, sa_scr, m_nxt)
        accumulate(j + 1, sb_scr, m_cur, m_nxt)
        return (*m_nxt, *m_nn)

    npairs = (nkv - 1) // 2
    carry = lax.fori_loop(0, npairs, pair, (*m_init, *m_first))
    m_prev, m_cur = list(carry[:2]), list(carry[2:])
    if nkv - 1 - 2 * npairs == 1:
        m_nxt = scores(nkv - 1, sb_scr, m_cur)
        accumulate(nkv - 2, sa_scr, m_prev, m_cur)
        accumulate(nkv - 1, sb_scr, m_cur, m_nxt)
    else:
        accumulate(nkv - 1, sa_scr, m_prev, m_cur)

    lp = lam_ref[...]
    lam = (jnp.exp(jnp.sum(lp[0:1] * lp[1:2], axis=-1, keepdims=True))
           - jnp.exp(jnp.sum(lp[2:3] * lp[3:4], axis=-1, keepdims=True)) + LAM_INIT)
    a0 = acc_scr[0]
    a1 = acc_scr[1]
    o = a0[:, :DIFF_VD] / a0[:, DIFF_VD:] - lam * (a1[:, :DIFF_VD] / a1[:, DIFF_VD:])
    o_ref[0] = (_rms(o, sub_ref[...]) * (1.0 - LAM_INIT)).astype(BF16)


def _pick_tile(n, cands):
    for c in cands:
        if n % c == 0:
            return c
    raise ValueError(f"no tile for {n}")


def _attention(q, k_all, v_all, lamp, subln):
    b, s, aw = q.shape
    sk = k_all.shape[1]
    tq = min(1024, s)
    tk = _pick_tile(sk, (768, 512, 384, 256, 128))
    kern = functools.partial(_attn_kernel, tk=tk, nkv=sk // tk)
    return pl.pallas_call(
        kern,
        grid=(b, DIFF_HEADS, s // tq),
        in_specs=[pl.BlockSpec((1, tq, LANES), lambda i, h, j: (i, j, h)),
                  pl.BlockSpec((1, sk, LANES), lambda i, h, j: (i, 0, h)),
                  pl.BlockSpec((1, sk, LANES), lambda i, h, j: (i, 0, h)),
                  pl.BlockSpec(lamp.shape, lambda i, h, j: (0, 0)),
                  pl.BlockSpec((1, DIFF_VD), lambda i, h, j: (0, 0))],
        out_specs=pl.BlockSpec((1, tq, LANES), lambda i, h, j: (i, j, h)),
        out_shape=jax.ShapeDtypeStruct((b, s, aw), BF16),
        scratch_shapes=[pltpu.VMEM((2, tq, LANES), BF16),
                        pltpu.VMEM((2, tq, tk), F32),
                        pltpu.VMEM((2, tq, tk), F32),
                        pltpu.VMEM((2, tq, 2 * DIFF_VD), F32)],
        compiler_params=_cparams(("parallel", "parallel", "parallel")),
        name="attn",
    )(q, k_all, v_all, lamp, subln)


def _filt_kernel(f_ref, w1_ref, b1_ref, w2_ref, b2_ref, w3_ref, b3_ref, fr_ref, w4_ref, dl_ref, o_ref):
    f = f_ref[...]
    fr = fr_ref[...]
    dot = functools.partial(jnp.dot, preferred_element_type=F32, precision=HIGHEST)
    h = jnp.sin(fr * (dot(f, w1_ref[...]) + b1_ref[...]))
    h = jnp.sin(fr * (dot(h, w2_ref[...]) + b2_ref[...]))
    h = jnp.sin(fr * (dot(h, w3_ref[...]) + b3_ref[...]))
    t = f[:, 0:1]
    valid = f[:, 2 * HY_BANDS + 1:2 * HY_BANDS + 2]
    o_ref[...] = dot(h, w4_ref[0]) * jnp.exp(-t * dl_ref[0]) * valid


def _hyena_filters(feat, p, w4r, deltas):
    n = feat.shape[0]
    tr = min(1024, n // 2)
    half_steps = (n // 2) // tr
    fo = p["hy_w2"].shape[0]
    cols = w4r.shape[2]
    const = lambda shape: pl.BlockSpec(shape, lambda i: (0,) * len(shape))
    w1 = jnp.zeros((LANES, fo), F32).at[:p["hy_w1"].shape[0]].set(p["hy_w1"])
    return pl.pallas_call(
        _filt_kernel,
        grid=(n // tr,),
        in_specs=[pl.BlockSpec((tr, LANES), lambda i: (i, 0)),
                  const((LANES, fo)), const((1, fo)), const((fo, fo)), const((1, fo)),
                  const((fo, fo)), const((1, fo)), const((1, fo)),
                  pl.BlockSpec((1, fo, cols), lambda i: (i // half_steps, 0, 0)),
                  pl.BlockSpec((1, 1, cols), lambda i: (i // half_steps, 0, 0))],
        out_specs=pl.BlockSpec((tr, cols), lambda i: (i, 0)),
        out_shape=jax.ShapeDtypeStruct((n, cols), F32),
        compiler_params=_cparams(("parallel",)),
        name="filt",
    )(feat, w1, p["hy_b1"].reshape(1, fo), p["hy_w2"], p["hy_b2"].reshape(1, fo),
      p["hy_w3"], p["hy_b3"].reshape(1, fo), p["hy_freq"].reshape(1, fo), w4r, deltas)


def _dft_tables(n1):
    n2 = LANES
    n = n1 * n2
    h1 = n1 // 2
    i2 = jnp.arange(n2, dtype=jnp.int32)
    i1 = jnp.arange(n1, dtype=jnp.int32)
    idx = (i2[:, None, None] * i1[None, :, None] + n2 * i1[None, :, None] * i1[None, None, :]) % n
    ang = idx.astype(F32) * (-2.0 * math.pi / n)
    gr, gi = jnp.cos(ang), jnp.sin(ang)
    grh, gih = gr[:, :, :h1], gi[:, :, :h1]
    g1_sig = jnp.concatenate([jnp.concatenate([grh, -gih], 2), jnp.concatenate([gih, grh], 2)], 1)
    g1_flt = jnp.concatenate([gr, gi], 1)
    qr, qi = jnp.swapaxes(grh, 1, 2), -jnp.swapaxes(gih, 1, 2)
    g3 = jnp.concatenate([jnp.concatenate([qr, -qi], 2), jnp.concatenate([qi, qr], 2)], 1)
    a2 = ((i2[:, None] * i2[None, :]) % n2).astype(F32) * (-2.0 * math.pi / n2)
    fr, fi = jnp.cos(a2), jnp.sin(a2)
    f2 = jnp.concatenate([jnp.concatenate([fr, -fi], 1), jnp.concatenate([fi, fr], 1)], 0)
    f2i = jnp.concatenate([jnp.concatenate([fr, fi], 1), jnp.concatenate([-fi, fr], 1)], 0)
    return (g1_sig.astype(BF16), g1_flt.astype(BF16), g3.astype(BF16), f2.astype(BF16), f2i.astype(BF16))


def _a_pitch(n1):
    return 2 * n1 + SUBLANES


def _stage1(zy_ref, a_ref, g_ref, step, it, n1):
    pa = _a_pitch(n1)

    def body(i, c):
        n2 = step * it + i
        rhs = zy_ref[pl.ds(n2, n1, stride=Z_PITCH), :].astype(BF16)
        a_ref[pl.ds(pl.multiple_of(n2 * pa, SUBLANES), 2 * n1), :] = jnp.dot(
            g_ref[i], rhs, preferred_element_type=F32)
        return c

    lax.fori_loop(0, it, body, 0, unroll=DFT_UNROLL)


def _gather_k1(a_ref, k1, n1):
    pa = _a_pitch(n1)
    re = a_ref[pl.ds(k1, LANES, stride=pa), :]
    im = a_ref[pl.ds(n1 + k1, LANES, stride=pa), :]
    return jnp.concatenate([re, im], axis=0).astype(BF16)


def _load_blocks(dst_ref, src, nblk, base):
    def body(i, c):
        r0 = pl.multiple_of(i * LANES, LANES)
        d0 = pl.multiple_of((base + i) * Z_PITCH, SUBLANES)
        dst_ref[pl.ds(d0, LANES), :] = src(r0).astype(F32)
        return c

    lax.fori_loop(0, nblk, body, 0)


def _fdft_kernel(k_ref, g1_ref, f2_ref, o_ref, zy_ref, a_ref, *, n1, it1, itm, s1):
    t = pl.program_id(1)

    @pl.when(t == 0)
    def _():
        _load_blocks(zy_ref, lambda r0: k_ref[pl.ds(r0, LANES), :], n1, 0)

    @pl.when(t < s1)
    def _():
        _stage1(zy_ref, a_ref, g1_ref, t, it1, n1)

    @pl.when(t >= s1)
    def _():
        f2 = f2_ref[...]
        scale = 1.0 / (n1 * LANES)

        def body(i, c):
            k1 = (t - s1) * itm + i
            y = jnp.dot(f2, _gather_k1(a_ref, k1, n1), preferred_element_type=F32)
            o_ref[0, i] = (y * scale).astype(BF16)
            return c

        lax.fori_loop(0, itm, body, 0, unroll=DFT_UNROLL)


def _filter_spectra(kfull, g1_flt, f2, n1):
    n, cols = kfull.shape
    nslab = cols // LANES
    it1 = 16
    itm = min(16, n1)
    s1 = LANES // it1
    sm = n1 // itm
    kern = functools.partial(_fdft_kernel, n1=n1, it1=it1, itm=itm, s1=s1)
    return pl.pallas_call(
        kern,
        grid=(nslab, s1 + sm),
        in_specs=[pl.BlockSpec((n, LANES), lambda s, t: (0, s)),
                  pl.BlockSpec((it1, 2 * n1, n1), lambda s, t: (jnp.minimum(t, s1 - 1), 0, 0)),
                  pl.BlockSpec((2 * LANES, 2 * LANES), lambda s, t: (0, 0))],
        out_specs=pl.BlockSpec((1, itm, 2 * LANES, LANES), lambda s, t: (s, jnp.maximum(t - s1, 0), 0, 0)),
        out_shape=jax.ShapeDtypeStruct((nslab, n1, 2 * LANES, LANES), BF16),
        scratch_shapes=[pltpu.VMEM((n1 * Z_PITCH, LANES), F32),
                        pltpu.VMEM((LANES * _a_pitch(n1), LANES), F32)],
        compiler_params=_cparams(("parallel", "arbitrary")),
        name="fdft",
    )(kfull, g1_flt, f2)


def _hconv_kernel(x_ref, w_ref, b_ref, o_ref):
    x = x_ref[0].astype(F32)
    s = x.shape[0]
    row = lax.broadcasted_iota(jnp.int32, x.shape, 0)
    prev = jnp.where(row == 0, 0.0, pltpu.roll(x, 1, 0))
    nxt = jnp.where(row == s - 1, 0.0, pltpu.roll(x, s - 1, 0))
    w = w_ref[...]
    o_ref[0] = (prev * w[0:1] + x * w[1:2] + nxt * w[2:3] + b_ref[...]).astype(BF16)


def _hyena_short_conv(hy, w, bias):
    b, s, c = hy.shape
    return pl.pallas_call(
        _hconv_kernel,
        grid=(b, c // LANES),
        in_specs=[pl.BlockSpec((1, s, LANES), lambda i, j: (i, 0, j)),
                  pl.BlockSpec((3, LANES), lambda i, j: (0, j)),
                  pl.BlockSpec((1, LANES), lambda i, j: (0, j))],
        out_specs=pl.BlockSpec((1, s, LANES), lambda i, j: (i, 0, j)),
        out_shape=jax.ShapeDtypeStruct((b, s, c), BF16),
        compiler_params=_cparams(("parallel", "parallel")),
        name="hconv",
    )(hy, w, bias.reshape(1, c))


def _hyena_kernel(z_ref, gate_ref, skip_ref, g1_ref, kf_ref, g3_ref, f2_ref, f2i_ref, o_ref,
                  zy_ref, a_ref, *, n1, it1, itm, s1, sm):
    t = pl.program_id(1)
    h1 = n1 // 2
    pa = _a_pitch(n1)

    @pl.when(t == 0)
    def _():
        for bb in range(2):
            _load_blocks(zy_ref, lambda r0, bb=bb: z_ref[bb, pl.ds(r0, LANES), :], h1, bb * h1)

    @pl.when(t < s1)
    def _():
        _stage1(zy_ref, a_ref, g1_ref, t, it1, n1)

    @pl.when(jnp.logical_and(t >= s1, t < s1 + sm))
    def _():
        f2 = f2_ref[...]
        f2i = f2i_ref[...]

        def body(i, c):
            k1 = (t - s1) * itm + i
            y = jnp.dot(f2, _gather_k1(a_ref, k1, n1), preferred_element_type=F32)
            kf = kf_ref[0, i].astype(F32)
            yr, yi = y[:LANES], y[LANES:]
            kr, ki = kf[:LANES], kf[LANES:]
            prod = jnp.concatenate([yr * kr - yi * ki, yr * ki + yi * kr], axis=0).astype(BF16)
            r = jnp.dot(f2i, prod, preferred_element_type=F32)
            a_ref[pl.ds(k1, LANES, stride=pa), :] = r[:LANES]
            a_ref[pl.ds(n1 + k1, LANES, stride=pa), :] = r[LANES:]
            return c

        lax.fori_loop(0, itm, body, 0, unroll=DFT_UNROLL)

    @pl.when(t >= s1 + sm)
    def _():
        def body(i, c):
            n2 = (t - s1 - sm) * it1 + i
            rhs = a_ref[pl.ds(pl.multiple_of(n2 * pa, SUBLANES), 2 * n1), :].astype(BF16)
            zy_ref[pl.ds(n2, n1, stride=Z_PITCH), :] = jnp.dot(g3_ref[i], rhs, preferred_element_type=F32)
            return c

        lax.fori_loop(0, it1, body, 0, unroll=DFT_UNROLL)

    @pl.when(t == s1 + sm + s1 - 1)
    def _():
        skip = skip_ref[...]
        for bb in range(2):
            def body(i, c, bb=bb):
                r0 = pl.multiple_of(i * LANES, LANES)
                d0 = pl.multiple_of((bb * h1 + i) * Z_PITCH, SUBLANES)
                y = zy_ref[pl.ds(d0, LANES), :]
                zz = z_ref[bb, pl.ds(r0, LANES), :].astype(F32)
                gg = gate_ref[bb, pl.ds(r0, LANES), :].astype(F32)
                o_ref[bb, pl.ds(r0, LANES), :] = (gg * (y + skip * zz)).astype(BF16)
                return c

            lax.fori_loop(0, h1, body, 0)


def _hyena_order(z, z_off, gate, gate_off, skip, kf, kf_off, tables, n1):
    g1_sig, _, g3, f2, f2i = tables
    b, s, _ = z.shape
    width = skip.shape[1]
    nslab = width // LANES
    it1 = 16
    itm = min(16, n1)
    s1 = LANES // it1
    sm = n1 // itm
    steps = 2 * s1 + sm
    kern = functools.partial(_hyena_kernel, n1=n1, it1=it1, itm=itm, s1=s1, sm=sm)
    one = pl.Buffered(1)
    return pl.pallas_call(
        kern,
        grid=((b // 2) * nslab, steps),
        in_specs=[pl.BlockSpec((2, s, LANES), lambda g, t: (g // nslab, 0, z_off + g % nslab), pipeline_mode=one),
                  pl.BlockSpec((2, s, LANES), lambda g, t: (g // nslab, 0, gate_off + g % nslab),
                               pipeline_mode=one),
                  pl.BlockSpec((1, LANES), lambda g, t: (0, g % nslab)),
                  pl.BlockSpec((it1, 2 * n1, n1), lambda g, t: (jnp.minimum(t, s1 - 1), 0, 0)),
                  pl.BlockSpec((1, itm, 2 * LANES, LANES),
                               lambda g, t: (kf_off + g % nslab, jnp.clip(t - s1, 0, sm - 1), 0, 0)),
                  pl.BlockSpec((it1, n1, 2 * n1), lambda g, t: (jnp.clip(t - s1 - sm, 0, s1 - 1), 0, 0)),
                  pl.BlockSpec((2 * LANES, 2 * LANES), lambda g, t: (0, 0)),
                  pl.BlockSpec((2 * LANES, 2 * LANES), lambda g, t: (0, 0))],
        out_specs=pl.BlockSpec((2, s, LANES), lambda g, t: (g // nslab, 0, g % nslab)),
        out_shape=jax.ShapeDtypeStruct((b, s, width), BF16),
        scratch_shapes=[pltpu.VMEM((n1 * Z_PITCH, LANES), F32),
                        pltpu.VMEM((LANES * _a_pitch(n1), LANES), F32)],
        compiler_params=_cparams(("parallel", "arbitrary")),
        name="hyena",
    )(z, gate, skip, g1_sig, kf, g3, f2, f2i)


def _outproj_kernel(a_ref, z_ref, x_ref, mod_ref, hn_ref, w_ref, o_ref):
    aw = ATTN_WIDTH
    zn = _rms(z_ref[0].astype(F32), hn_ref[...]).astype(BF16)
    y = (jnp.dot(a_ref[0], w_ref[:aw, :], preferred_element_type=F32)
         + jnp.dot(zn, w_ref[aw:, :], preferred_element_type=F32))
    o_ref[0] = x_ref[0] + mod_ref[0][2:3] * y


def _outproj(attn, hz, x, modr, hy_norm, w_out_bf):
    b, s, d = x.shape
    ts = min(512, s)
    aw = attn.shape[2]
    hw = hz.shape[2]
    return pl.pallas_call(
        _outproj_kernel,
        grid=(b, s // ts),
        in_specs=[pl.BlockSpec((1, ts, aw), lambda i, j: (i, j, 0)),
                  pl.BlockSpec((1, ts, hw), lambda i, j: (i, j, 0)),
                  pl.BlockSpec((1, ts, d), lambda i, j: (i, j, 0)),
                  pl.BlockSpec((1, 8, d), lambda i, j: (i, 0, 0)),
                  pl.BlockSpec((1, hw), lambda i, j: (0, 0)),
                  pl.BlockSpec(w_out_bf.shape, lambda i, j: (0, 0))],
        out_specs=pl.BlockSpec((1, ts, d), lambda i, j: (i, j, 0)),
        out_shape=jax.ShapeDtypeStruct((b, s, d), F32),
        compiler_params=_cparams(("parallel", "parallel")),
        name="outproj",
    )(attn, hz, x, modr, hy_norm, w_out_bf)


def _ffn_kernel(x_ref, xp_ref, xn_ref, mod_ref, g_ref, wu_ref, cw_ref, cb_ref, wd_ref, fg_ref, o_ref,
                *, dff, cf):
    j = pl.program_id(1)
    last = pl.num_programs(1) - 1
    xm = x_ref[0]
    ts = xm.shape[0]
    m = mod_ref[0]
    xa = jnp.concatenate([xp_ref[0], xm, xn_ref[0]], axis=0)
    h = (_rms(xa, g_ref[...]) * (1.0 + m[4:5]) + m[3:4]).astype(BF16)
    hm = h[SUBLANES:SUBLANES + ts]
    rows = ts + 2 * SUBLANES
    row = lax.broadcasted_iota(jnp.int32, (ts, 1), 0)
    no_prev = jnp.logical_and(j == 0, row == 0)
    no_next = jnp.logical_and(j == last, row == ts - 1)
    acc = jnp.zeros((ts, xm.shape[1]), F32)
    for c in range(dff // cf):
        c0 = c * cf
        a = jnp.dot(h, wu_ref[:, c0:c0 + cf], preferred_element_type=F32)
        gate = jnp.dot(hm, wu_ref[:, dff + c0:dff + c0 + cf], preferred_element_type=F32)
        prev = jnp.where(no_prev, 0.0, pltpu.roll(a, 1, 0)[SUBLANES:SUBLANES + ts])
        nxt = jnp.where(no_next, 0.0, pltpu.roll(a, rows - 1, 0)[SUBLANES:SUBLANES + ts])
        cw = cw_ref[:, c0:c0 + cf]
        ac = prev * cw[0:1] + a[SUBLANES:SUBLANES + ts] * cw[1:2] + nxt * cw[2:3] + cb_ref[:, c0:c0 + cf]
        gl = 0.5 * ac * (1.0 + lax.erf(ac * (2.0 ** -0.5)))
        acc = acc + jnp.dot((gl * gate).astype(BF16), wd_ref[c0:c0 + cf, :], preferred_element_type=F32)
    y = xm + m[5:6] * acc
    o_ref[0] = _rms(y, fg_ref[...])


def _ffn(x1, modr, norm_g, wu_bf, conv_w, conv_b, wd_bf, final_g):
    b, s, d = x1.shape
    dff = wd_bf.shape[0]
    ts = min(512, s)
    nh = ts // SUBLANES
    nblk8 = s // SUBLANES
    cf = dff // 2 if (dff // 2) % LANES == 0 else dff
    kern = functools.partial(_ffn_kernel, dff=dff, cf=cf)
    one = pl.Buffered(1)
    return pl.pallas_call(
        kern,
        grid=(b, s // ts),
        in_specs=[pl.BlockSpec((1, ts, d), lambda i, j: (i, j, 0)),
                  pl.BlockSpec((1, SUBLANES, d), lambda i, j: (i, jnp.maximum(j * nh - 1, 0), 0)),
                  pl.BlockSpec((1, SUBLANES, d), lambda i, j: (i, jnp.minimum((j + 1) * nh, nblk8 - 1), 0)),
                  pl.BlockSpec((1, 8, d), lambda i, j: (i, 0, 0)),
                  pl.BlockSpec((1, d), lambda i, j: (0, 0)),
                  pl.BlockSpec(wu_bf.shape, lambda i, j: (0, 0), pipeline_mode=one),
                  pl.BlockSpec(conv_w.shape, lambda i, j: (0, 0)),
                  pl.BlockSpec((1, dff), lambda i, j: (0, 0)),
                  pl.BlockSpec(wd_bf.shape, lambda i, j: (0, 0), pipeline_mode=one),
                  pl.BlockSpec((1, d), lambda i, j: (0, 0))],
        out_specs=pl.BlockSpec((1, ts, d), lambda i, j: (i, j, 0)),
        out_shape=jax.ShapeDtypeStruct((b, s, d), F32),
        compiler_params=_cparams(("parallel", "parallel")),
        name="ffn",
    )(x1, x1, x1, modr, norm_g, wu_bf, conv_w, conv_b.reshape(1, dff), wd_bf, final_g)


def _rope_tables(s):
    nf = DIFF_DH // 4
    inv = ROPE_BASE ** (-jnp.arange(nf, dtype=F32) / nf)
    t = jnp.arange(s, dtype=jnp.int32)
    row = (t // GRID_W).astype(F32)[:, None] * inv
    col = (t % GRID_W).astype(F32)[:, None] * inv
    cos = jnp.concatenate([jnp.cos(row), jnp.cos(row), jnp.cos(col), jnp.cos(col)], axis=1)
    sin = jnp.concatenate([-jnp.sin(row), jnp.sin(row), -jnp.sin(col), jnp.sin(col)], axis=1)
    return jnp.tile(cos, (1, LANES // DIFF_DH)), jnp.tile(sin, (1, LANES // DIFF_DH))


def _filter_features(l):
    pos = jnp.concatenate([jnp.arange(l, dtype=jnp.int32), l - jnp.arange(l, dtype=jnp.int32)])
    valid = jnp.ones((2 * l,), F32).at[l].set(0.0)
    pos = jnp.where(pos == l, 0, pos).astype(F32)[:, None]
    tt = pos / (l - 1)
    w = 2.0 * math.pi * pos / l
    f = jnp.linspace(1e-4, HY_BANDS - 1, HY_BANDS, dtype=F32)[None, :]
    feat = jnp.concatenate([tt, jnp.cos(f * w), -jnp.sin(f * w), valid[:, None]], axis=-1)
    return jnp.pad(feat, ((0, 0), (0, LANES - feat.shape[1])))


def kernel(x, c, ctx, c_ctx, w_mod, b_mod, norm_mix, norm_ffn, w_in, lam_q1, lam_k1, lam_q2, lam_k2, subln, hy_conv_w, hy_conv_b, hy_w1, hy_b1, hy_w2, hy_b2, hy_w3, hy_b3, hy_w4, hy_freq, hy_skip, hy_norm, w_out, ffn_w_up, ffn_conv_w, ffn_conv_b, ffn_w_down, final_norm):
    b, s, d = x.shape
    assert w_mod.shape[0] == 1 and b % 2 == 0 and (2 * s) % (LANES * 16) == 0
    aw = ATTN_WIDTH
    hw = hy_skip.shape[2]
    n1 = 2 * s // LANES

    rows = ((b + 1 + 7) // 8) * 8
    cvec = jnp.zeros((rows, d), F32).at[:b].set(c).at[b].set(c_ctx)
    mod = _modulation(cvec, w_mod[0], b_mod[0])[:b + 1]
    modr = jnp.pad(mod.reshape(b + 1, N_MOD, d), ((0, 0), (0, 8 - N_MOD), (0, 0)))

    w_in_bf = w_in[0].astype(BF16)
    cos_t, sin_t = _rope_tables(s)
    g_mix = norm_mix[0].reshape(1, d)
    q, k_x, v_x, hy = _inproj(x, modr, g_mix, w_in_bf, cos_t, sin_t)
    k_c, v_c = _ctxproj(ctx, modr, g_mix, w_in_bf[:, aw:3 * aw])
    k_all = jnp.concatenate([k_c, k_x], axis=1)
    v_all = jnp.concatenate([v_c, v_x], axis=1)
    lamp = jnp.concatenate([lam_q1, lam_k1, lam_q2, lam_k2], axis=0)
    attn = _attention(q, k_all, v_all, lamp, subln[0].reshape(1, DIFF_VD))

    p = {"hy_w1": hy_w1[0], "hy_b1": hy_b1[0], "hy_w2": hy_w2[0], "hy_b2": hy_b2[0],
         "hy_w3": hy_w3[0], "hy_b3": hy_b3[0], "hy_freq": hy_freq[0]}
    fo = hy_w4.shape[1]
    w4r = hy_w4[0].reshape(fo, HY_ORDER, 2, hw).transpose(2, 0, 1, 3).reshape(2, fo, HY_ORDER * hw)
    min_decay = math.log(HY_DECAY_TARGET) / HY_FAST_DECAY_PCT
    max_decay = math.log(HY_DECAY_TARGET) / HY_SLOW_DECAY_PCT
    deltas = jnp.abs(jnp.linspace(min_decay, max_decay, HY_ORDER * 2 * hw, dtype=F32))
    deltas = deltas.reshape(HY_ORDER, 2, hw).transpose(1, 0, 2).reshape(2, 1, HY_ORDER * hw)
    kfull = _hyena_filters(_filter_features(s), p, w4r, deltas)
    tables = _dft_tables(n1)
    kf = _filter_spectra(kfull, tables[1], tables[3], n1)

    u = _hyena_short_conv(hy, hy_conv_w[0], hy_conv_b[0])
    nslab = hw // LANES
    z = _hyena_order(u, 0, u, nslab, hy_skip[0, 0:1], kf, 0, tables, n1)
    z = _hyena_order(z, 0, u, 2 * nslab, hy_skip[0, 1:2], kf, nslab, tables, n1)

    x1 = _outproj(attn, z, x, modr, hy_norm[0].reshape(1, hw), w_out[0].astype(BF16))
    return _ffn(x1, modr, norm_ffn[0].reshape(1, d), ffn_w_up[0].astype(BF16), ffn_conv_w[0],
                ffn_conv_b[0], ffn_w_down[0].astype(BF16), final_norm.reshape(1, d))
```

```python
import functools
import math

import jax
import jax.numpy as jnp
from jax import lax
from jax.experimental import pallas as pl
from jax.experimental.pallas import tpu as pltpu

F32 = jnp.float32
BF16 = jnp.bfloat16
HIGHEST = lax.Precision.HIGHEST

EPS = 1e-6
N_MOD = 6
GRID_W = 64
DIFF_HEADS = 4
DIFF_DH = 64
DIFF_VD = 128
ATTN_WIDTH = DIFF_HEADS * DIFF_VD
HY_ORDER = 2
HY_BANDS = 16
HY_DECAY_TARGET = 1e-2
HY_FAST_DECAY_PCT = 0.3
HY_SLOW_DECAY_PCT = 1.5
ROPE_BASE = 10000.0
LAM_INIT = 0.8 - 0.6 * math.exp(-0.3 * 0)

LANES = 128
SUBLANES = 8
MXU_DIM = 256
Z_PITCH = LANES + SUBLANES
DFT_UNROLL = 8
VMEM_LIMIT = 56 * 1024 * 1024
NEG_BIG = -1e30
NT_DIMS = (((1,), (1,)), ((), ()))
ONES_ROWS = 16
Q_SCALE = DIFF_DH ** -0.5 * math.log2(math.e)


def _cparams(sem):
    return pltpu.CompilerParams(dimension_semantics=sem, vmem_limit_bytes=VMEM_LIMIT)


def _rms(x, g):
    return x * lax.rsqrt(jnp.mean(x * x, axis=-1, keepdims=True) + EPS) * g


def _mod_kernel(c_ref, w_ref, b_ref, o_ref):
    c = c_ref[...]
    s = c * jax.nn.sigmoid(c)
    o_ref[...] = jnp.dot(s, w_ref[...], preferred_element_type=F32, precision=HIGHEST) + b_ref[...]


def _modulation(cvec, w_mod, b_mod):
    rows, d = cvec.shape
    cols = w_mod.shape[1]
    bc = 1024
    return pl.pallas_call(
        _mod_kernel,
        grid=(cols // bc,),
        in_specs=[pl.BlockSpec((rows, d), lambda j: (0, 0)),
                  pl.BlockSpec((d, bc), lambda j: (0, j)),
                  pl.BlockSpec((1, bc), lambda j: (0, j))],
        out_specs=pl.BlockSpec((rows, bc), lambda j: (0, j)),
        out_shape=jax.ShapeDtypeStruct((rows, cols), F32),
        compiler_params=_cparams(("arbitrary",)),
        name="mod",
    )(cvec, w_mod, b_mod.reshape(1, cols))


def _rope(t, cos, sin_signed, first_half):
    sw = jnp.where(first_half, pltpu.roll(t, LANES - 16, 1), pltpu.roll(t, 16, 1))
    return t * cos + sw * sin_signed


def _halo_specs(ts, s, d):
    nh = ts // SUBLANES
    nblk8 = s // SUBLANES
    return [pl.BlockSpec((1, ts, d), lambda i, j: (i, j, 0)),
            pl.BlockSpec((1, SUBLANES, d), lambda i, j: (i, jnp.maximum(j * nh - 1, 0), 0)),
            pl.BlockSpec((1, SUBLANES, d), lambda i, j: (i, jnp.minimum((j + 1) * nh, nblk8 - 1), 0))]


def _conv3(a, w, bias, ts, no_prev, no_next):
    rows = ts + 2 * SUBLANES
    prev = jnp.where(no_prev, 0.0, pltpu.roll(a, 1, 0)[SUBLANES:SUBLANES + ts])
    nxt = jnp.where(no_next, 0.0, pltpu.roll(a, rows - 1, 0)[SUBLANES:SUBLANES + ts])
    return prev * w[0:1] + a[SUBLANES:SUBLANES + ts] * w[1:2] + nxt * w[2:3] + bias


def _edge_masks(ts):
    j = pl.program_id(1)
    row = lax.broadcasted_iota(jnp.int32, (ts, 1), 0)
    no_prev = jnp.logical_and(j == 0, row == 0)
    no_next = jnp.logical_and(j == pl.num_programs(1) - 1, row == ts - 1)
    return no_prev, no_next


def _inproj_kernel(x_ref, xp_ref, xn_ref, mod_ref, g_ref, w_ref, wvt_ref, cos_ref, sin_ref, cw_ref, cb_ref,
                   q_ref, k_ref, vt_ref, u_ref):
    ts = x_ref.shape[1]
    m = mod_ref[0]
    xa = jnp.concatenate([xp_ref[0], x_ref[0], xn_ref[0]], axis=0)
    ha = (_rms(xa, g_ref[...]) * (1.0 + m[1:2]) + m[0:1]).astype(BF16)
    h = ha[SUBLANES:SUBLANES + ts]
    cos = cos_ref[...]
    sin = sin_ref[...]
    lane = lax.broadcasted_iota(jnp.int32, cos.shape, 1)
    first_half = (lane % 32) < 16
    aw = ATTN_WIDTH
    q = jnp.dot(h, w_ref[:, :aw], preferred_element_type=F32)
    k = jnp.dot(h, w_ref[:, aw:2 * aw], preferred_element_type=F32)
    for s in range(aw // LANES):
        c0 = s * LANES
        q_ref[0, :, c0:c0 + LANES] = (_rope(q[:, c0:c0 + LANES], cos, sin, first_half) * Q_SCALE).astype(BF16)
        k_ref[0, :, c0:c0 + LANES] = _rope(k[:, c0:c0 + LANES], cos, sin, first_half).astype(BF16)
    vt_ref[0] = lax.dot_general(wvt_ref[...], h, NT_DIMS, preferred_element_type=F32).astype(BF16)
    hy = jnp.dot(ha, w_ref[:, 3 * aw:], preferred_element_type=F32)
    u_ref[0] = _conv3(hy, cw_ref[...], cb_ref[...], ts, *_edge_masks(ts)).astype(BF16)


def _inproj(x, modr, norm_g, w_in_bf, w_vt_bf, cos_t, sin_t, conv_w, conv_b):
    b, s, d = x.shape
    ts = min(512, s)
    aw = ATTN_WIDTH
    hyc = w_in_bf.shape[1] - 3 * aw
    return pl.pallas_call(
        _inproj_kernel,
        grid=(b, s // ts),
        in_specs=_halo_specs(ts, s, d) + [
                  pl.BlockSpec((1, 8, d), lambda i, j: (i, 0, 0)),
                  pl.BlockSpec((1, d), lambda i, j: (0, 0)),
                  pl.BlockSpec(w_in_bf.shape, lambda i, j: (0, 0)),
                  pl.BlockSpec(w_vt_bf.shape, lambda i, j: (0, 0)),
                  pl.BlockSpec((ts, LANES), lambda i, j: (j, 0)),
                  pl.BlockSpec((ts, LANES), lambda i, j: (j, 0)),
                  pl.BlockSpec(conv_w.shape, lambda i, j: (0, 0)),
                  pl.BlockSpec((1, hyc), lambda i, j: (0, 0))],
        out_specs=[pl.BlockSpec((1, ts, aw), lambda i, j: (i, j, 0)),
                   pl.BlockSpec((1, ts, aw), lambda i, j: (i, j, 0)),
                   pl.BlockSpec((1, aw, ts), lambda i, j: (i, 0, j)),
                   pl.BlockSpec((1, ts, hyc), lambda i, j: (i, j, 0))],
        out_shape=[jax.ShapeDtypeStruct((b, s, aw), BF16)] * 2 + [jax.ShapeDtypeStruct((b, aw, s), BF16),
                                                                   jax.ShapeDtypeStruct((b, s, hyc), BF16)],
        compiler_params=_cparams(("parallel", "parallel")),
        name="inproj",
    )(x, x, x, modr, norm_g, w_in_bf, w_vt_bf, cos_t, sin_t, conv_w, conv_b.reshape(1, hyc))


def _ctxproj_kernel(x_ref, mod_ref, g_ref, wk_ref, wvt_ref, k_ref, vt_ref):
    m = mod_ref[0]
    h = (_rms(x_ref[0], g_ref[...]) * (1.0 + m[1:2]) + m[0:1]).astype(BF16)
    k_ref[0] = jnp.dot(h, wk_ref[...], preferred_element_type=F32).astype(BF16)
    vt_ref[0] = lax.dot_general(wvt_ref[...], h, NT_DIMS, preferred_element_type=F32).astype(BF16)


def _ctxproj(ctx, modr, norm_g, w_k_bf, w_vt_bf):
    b, s, d = ctx.shape
    aw = ATTN_WIDTH
    nb = modr.shape[0] - 1
    return pl.pallas_call(
        _ctxproj_kernel,
        grid=(b,),
        in_specs=[pl.BlockSpec((1, s, d), lambda i: (i, 0, 0)),
                  pl.BlockSpec((1, 8, d), lambda i: (nb, 0, 0)),
                  pl.BlockSpec((1, d), lambda i: (0, 0)),
                  pl.BlockSpec(w_k_bf.shape, lambda i: (0, 0)),
                  pl.BlockSpec(w_vt_bf.shape, lambda i: (0, 0))],
        out_specs=[pl.BlockSpec((1, s, aw), lambda i: (i, 0, 0)), pl.BlockSpec((1, aw, s), lambda i: (i, 0, 0))],
        out_shape=[jax.ShapeDtypeStruct((b, s, aw), BF16), jax.ShapeDtypeStruct((b, aw, s), BF16)],
        compiler_params=_cparams(("parallel",)),
        name="ctxproj",
    )(ctx, modr, norm_g, w_k_bf, w_vt_bf)


def _attn_kernel(q_ref, k_ref, vt_ref, lam_ref, sub_ref, o_ref, qm_scr, sa_scr, sb_scr, acc_scr, *, tk, nkv):
    q = q_ref[0]
    lane = lax.broadcasted_iota(jnp.int32, q.shape, 1)
    zero = jnp.zeros_like(q)
    qm_scr[0] = jnp.where(lane < DIFF_DH, q, zero)
    qm_scr[1] = jnp.where(lane >= DIFF_DH, q, zero)
    acc_scr[...] = jnp.zeros(acc_scr.shape, F32)
    tq = q.shape[0]
    ones = jnp.ones((ONES_ROWS, tk), BF16)

    def scores(j, s_scr, m_cur):
        kj = k_ref[0, pl.ds(pl.multiple_of(j * tk, tk), tk), :]
        m_next = []
        for mp in range(2):
            s = lax.dot_general(kj, qm_scr[mp], NT_DIMS, preferred_element_type=F32)
            s_scr[mp] = s
            m_next.append(jnp.maximum(m_cur[mp], jnp.max(s, axis=0, keepdims=True)))
        return m_next

    def accumulate(j, s_scr, m_prev, m_cur):
        vt = vt_ref[0, :, pl.ds(pl.multiple_of(j * tk, tk), tk)]
        vext = jnp.concatenate([vt, ones], axis=0)
        for mp in range(2):
            p = jnp.exp2(s_scr[mp] - m_cur[mp]).astype(BF16)
            alpha = jnp.exp2(m_prev[mp] - m_cur[mp])
            acc_scr[mp] = alpha * acc_scr[mp] + jnp.dot(vext, p, preferred_element_type=F32)

    m_init = [jnp.full((1, tq), NEG_BIG, F32)] * 2
    m_first = scores(0, sa_scr, m_init)

    def pair(i, carry):
        m_prev, m_cur = list(carry[:2]), list(carry[2:])
        j = 2 * i
        m_nxt = scores(j + 1, sb_scr, m_cur)
        accumulate(j, sa_scr, m_prev, m_cur)
        m_nn = scores(j + 2, sa_scr, m_nxt)
        accumulate(j + 1, sb_scr, m_cur, m_nxt)
        return (*m_nxt, *m_nn)

    npairs = (nkv - 1) // 2
    carry = lax.fori_loop(0, npairs, pair, (*m_init, *m_first))
    m_prev, m_cur = list(carry[:2]), list(carry[2:])
    if nkv - 1 - 2 * npairs == 1:
        m_nxt = scores(nkv - 1, sb_scr, m_cur)
        accumulate(nkv - 2, sa_scr, m_prev, m_cur)
        accumulate(nkv - 1, sb_scr, m_cur, m_nxt)
    else:
        accumulate(nkv - 1, sa_scr, m_prev, m_cur)

    lp = lam_ref[...]
    lam = (jnp.exp(jnp.sum(lp[0:1] * lp[1:2], axis=-1, keepdims=True))
           - jnp.exp(jnp.sum(lp[2:3] * lp[3:4], axis=-1, keepdims=True)) + LAM_INIT)
    a0 = acc_scr[0]
    a1 = acc_scr[1]
    vd = DIFF_VD
    o = a0[:vd] / a0[vd:vd + 1] - lam * (a1[:vd] / a1[vd:vd + 1])
    on = o * lax.rsqrt(jnp.mean(o * o, axis=0, keepdims=True) + EPS)
    o_ref[0] = (on.T * sub_ref[...] * (1.0 - LAM_INIT)).astype(BF16)


def _pick_tile(n, cands):
    for c in cands:
        if n % c == 0:
            return c
    raise ValueError(f"no tile for {n}")


def _attention(q, k_all, vt_all, lamp, subln):
    b, s, aw = q.shape
    sk = k_all.shape[1]
    tq = min(1024, s)
    tk = _pick_tile(sk, (768, 512, 384, 256, 128))
    kern = functools.partial(_attn_kernel, tk=tk, nkv=sk // tk)
    return pl.pallas_call(
        kern,
        grid=(b, DIFF_HEADS, s // tq),
        in_specs=[pl.BlockSpec((1, tq, LANES), lambda i, h, j: (i, j, h)),
                  pl.BlockSpec((1, sk, LANES), lambda i, h, j: (i, 0, h)),
                  pl.BlockSpec((1, DIFF_VD, sk), lambda i, h, j: (i, h, 0)),
                  pl.BlockSpec(lamp.shape, lambda i, h, j: (0, 0)),
                  pl.BlockSpec((1, DIFF_VD), lambda i, h, j: (0, 0))],
        out_specs=pl.BlockSpec((1, tq, LANES), lambda i, h, j: (i, j, h)),
        out_shape=jax.ShapeDtypeStruct((b, s, aw), BF16),
        scratch_shapes=[pltpu.VMEM((2, tq, LANES), BF16),
                        pltpu.VMEM((2, tk, tq), F32),
                        pltpu.VMEM((2, tk, tq), F32),
                        pltpu.VMEM((2, DIFF_VD + ONES_ROWS, tq), F32)],
        compiler_params=_cparams(("parallel", "parallel", "parallel")),
        name="attn",
    )(q, k_all, vt_all, lamp, subln)


def _filt_kernel(f_ref, w1_ref, b1_ref, w2_ref, b2_ref, w3_ref, b3_ref, fr_ref, w4_ref, dl_ref, o_ref):
    f = f_ref[...]
    fr = fr_ref[...]
    dot = functools.partial(jnp.dot, preferred_element_type=F32, precision=HIGHEST)
    h = jnp.sin(fr * (dot(f, w1_ref[...]) + b1_ref[...]))
    h = jnp.sin(fr * (dot(h, w2_ref[...]) + b2_ref[...]))
    h = jnp.sin(fr * (dot(h, w3_ref[...]) + b3_ref[...]))
    t = f[:, 0:1]
    valid = f[:, 2 * HY_BANDS + 1:2 * HY_BANDS + 2]
    o_ref[...] = dot(h, w4_ref[0]) * jnp.exp(-t * dl_ref[0]) * valid


def _hyena_filters(feat, p, w4r, deltas):
    n = feat.shape[0]
    tr = min(1024, n // 2)
    half_steps = (n // 2) // tr
    fo = p["hy_w2"].shape[0]
    cols = w4r.shape[2]
    const = lambda shape: pl.BlockSpec(shape, lambda i: (0,) * len(shape))
    w1 = jnp.zeros((LANES, fo), F32).at[:p["hy_w1"].shape[0]].set(p["hy_w1"])
    return pl.pallas_call(
        _filt_kernel,
        grid=(n // tr,),
        in_specs=[pl.BlockSpec((tr, LANES), lambda i: (i, 0)),
                  const((LANES, fo)), const((1, fo)), const((fo, fo)), const((1, fo)),
                  const((fo, fo)), const((1, fo)), const((1, fo)),
                  pl.BlockSpec((1, fo, cols), lambda i: (i // half_steps, 0, 0)),
                  pl.BlockSpec((1, 1, cols), lambda i: (i // half_steps, 0, 0))],
        out_specs=pl.BlockSpec((tr, cols), lambda i: (i, 0)),
        out_shape=jax.ShapeDtypeStruct((n, cols), F32),
        compiler_params=_cparams(("parallel",)),
        name="filt",
    )(feat, w1, p["hy_b1"].reshape(1, fo), p["hy_w2"], p["hy_b2"].reshape(1, fo),
      p["hy_w3"], p["hy_b3"].reshape(1, fo), p["hy_freq"].reshape(1, fo), w4r, deltas)


def _dft_tables(n1):
    n2 = LANES
    n = n1 * n2
    h1 = n1 // 2
    i2 = jnp.arange(n2, dtype=jnp.int32)
    i1 = jnp.arange(n1, dtype=jnp.int32)
    idx = (i2[:, None, None] * i1[None, :, None] + n2 * i1[None, :, None] * i1[None, None, :]) % n
    ang = idx.astype(F32) * (-2.0 * math.pi / n)
    gr, gi = jnp.cos(ang), jnp.sin(ang)
    grh, gih = gr[:, :, :h1], gi[:, :, :h1]
    g1_sig = jnp.concatenate([jnp.concatenate([grh, -gih], 2), jnp.concatenate([gih, grh], 2)], 1)
    g1_flt = jnp.concatenate([gr, gi], 1)
    qr, qi = jnp.swapaxes(grh, 1, 2), -jnp.swapaxes(gih, 1, 2)
    g3 = jnp.concatenate([jnp.concatenate([qr, -qi], 2), jnp.concatenate([qi, qr], 2)], 1)
    a2 = ((i2[:, None] * i2[None, :]) % n2).astype(F32) * (-2.0 * math.pi / n2)
    fr, fi = jnp.cos(a2), jnp.sin(a2)
    f2 = jnp.concatenate([jnp.concatenate([fr, -fi], 1), jnp.concatenate([fi, fr], 1)], 0)
    f2i = jnp.concatenate([jnp.concatenate([fr, fi], 1), jnp.concatenate([-fi, fr], 1)], 0)
    return (g1_sig.astype(BF16), g1_flt.astype(BF16), g3.astype(BF16), f2.astype(BF16), f2i.astype(BF16))


def _a_pitch(n1):
    return 2 * n1 + SUBLANES


def _stage1(zy_ref, a_ref, g_ref, step, it, n1):
    pa = _a_pitch(n1)

    def body(i, c):
        n2 = step * it + i
        rhs = zy_ref[pl.ds(n2, n1, stride=Z_PITCH), :].astype(BF16)
        a_ref[pl.ds(pl.multiple_of(n2 * pa, SUBLANES), 2 * n1), :] = jnp.dot(
            g_ref[i], rhs, preferred_element_type=F32)
        return c

    lax.fori_loop(0, it, body, 0, unroll=DFT_UNROLL)


def _gather_k1(a_ref, k1, n1):
    pa = _a_pitch(n1)
    re = a_ref[pl.ds(k1, LANES, stride=pa), :]
    im = a_ref[pl.ds(n1 + k1, LANES, stride=pa), :]
    return jnp.concatenate([re, im], axis=0).astype(BF16)


def _load_blocks(dst_ref, src, nblk, base):
    def body(i, c):
        r0 = pl.multiple_of(i * LANES, LANES)
        d0 = pl.multiple_of((base + i) * Z_PITCH, SUBLANES)
        dst_ref[pl.ds(d0, LANES), :] = src(r0).astype(F32)
        return c

    lax.fori_loop(0, nblk, body, 0)


def _fdft_kernel(k_ref, g1_ref, f2_ref, o_ref, zy_ref, a_ref, *, n1, it1, itm, s1):
    t = pl.program_id(1)

    @pl.when(t == 0)
    def _():
        _load_blocks(zy_ref, lambda r0: k_ref[pl.ds(r0, LANES), :], n1, 0)

    @pl.when(t < s1)
    def _():
        _stage1(zy_ref, a_ref, g1_ref, t, it1, n1)

    @pl.when(t >= s1)
    def _():
        f2 = f2_ref[...]
        scale = 1.0 / (n1 * LANES)

        def body(i, c):
            k1 = (t - s1) * itm + i
            y = jnp.dot(f2, _gather_k1(a_ref, k1, n1), preferred_element_type=F32)
            o_ref[0, i] = (y * scale).astype(BF16)
            return c

        lax.fori_loop(0, itm, body, 0, unroll=DFT_UNROLL)


def _filter_spectra(kfull, g1_flt, f2, n1):
    n, cols = kfull.shape
    nslab = cols // LANES
    it1 = 16
    itm = min(16, n1)
    s1 = LANES // it1
    sm = n1 // itm
    kern = functools.partial(_fdft_kernel, n1=n1, it1=it1, itm=itm, s1=s1)
    return pl.pallas_call(
        kern,
        grid=(nslab, s1 + sm),
        in_specs=[pl.BlockSpec((n, LANES), lambda s, t: (0, s)),
                  pl.BlockSpec((it1, 2 * n1, n1), lambda s, t: (jnp.minimum(t, s1 - 1), 0, 0)),
                  pl.BlockSpec((2 * LANES, 2 * LANES), lambda s, t: (0, 0))],
        out_specs=pl.BlockSpec((1, itm, 2 * LANES, LANES), lambda s, t: (s, jnp.maximum(t - s1, 0), 0, 0)),
        out_shape=jax.ShapeDtypeStruct((nslab, n1, 2 * LANES, LANES), BF16),
        scratch_shapes=[pltpu.VMEM((n1 * Z_PITCH, LANES), F32),
                        pltpu.VMEM((LANES * _a_pitch(n1), LANES), F32)],
        compiler_params=_cparams(("parallel", "arbitrary")),
        name="fdft",
    )(kfull, g1_flt, f2)


def _hyena_kernel(z_ref, gate_ref, skip_ref, g1_ref, kf_ref, g3_ref, f2_ref, f2i_ref, o_ref,
                  zy_ref, a_ref, *, n1, it1, itm, s1, sm):
    t = pl.program_id(1)
    h1 = n1 // 2
    pa = _a_pitch(n1)

    @pl.when(t == 0)
    def _():
        for bb in range(2):
            _load_blocks(zy_ref, lambda r0, bb=bb: z_ref[bb, pl.ds(r0, LANES), :], h1, bb * h1)

    @pl.when(t < s1)
    def _():
        _stage1(zy_ref, a_ref, g1_ref, t, it1, n1)

    @pl.when(jnp.logical_and(t >= s1, t < s1 + sm))
    def _():
        f2 = f2_ref[...]
        f2i = f2i_ref[...]

        def body(i, c):
            k1 = (t - s1) * itm + i
            y = jnp.dot(f2, _gather_k1(a_ref, k1, n1), preferred_element_type=F32)
            kf = kf_ref[0, i].astype(F32)
            yr, yi = y[:LANES], y[LANES:]
            kr, ki = kf[:LANES], kf[LANES:]
            prod = jnp.concatenate([yr * kr - yi * ki, yr * ki + yi * kr], axis=0).astype(BF16)
            r = jnp.dot(f2i, prod, preferred_element_type=F32)
            a_ref[pl.ds(k1, LANES, stride=pa), :] = r[:LANES]
            a_ref[pl.ds(n1 + k1, LANES, stride=pa), :] = r[LANES:]
            return c

        lax.fori_loop(0, itm, body, 0, unroll=DFT_UNROLL)

    @pl.when(t >= s1 + sm)
    def _():
        def body(i, c):
            n2 = (t - s1 - sm) * it1 + i
            rhs = a_ref[pl.ds(pl.multiple_of(n2 * pa, SUBLANES), 2 * n1), :].astype(BF16)
            zy_ref[pl.ds(n2, n1, stride=Z_PITCH), :] = jnp.dot(g3_ref[i], rhs, preferred_element_type=F32)
            return c

        lax.fori_loop(0, it1, body, 0, unroll=DFT_UNROLL)

    @pl.when(t == s1 + sm + s1 - 1)
    def _():
        skip = skip_ref[...]
        for bb in range(2):
            def body(i, c, bb=bb):
                r0 = pl.multiple_of(i * LANES, LANES)
                d0 = pl.multiple_of((bb * h1 + i) * Z_PITCH, SUBLANES)
                y = zy_ref[pl.ds(d0, LANES), :]
                zz = z_ref[bb, pl.ds(r0, LANES), :].astype(F32)
                gg = gate_ref[bb, pl.ds(r0, LANES), :].astype(F32)
                o_ref[bb, pl.ds(r0, LANES), :] = (gg * (y + skip * zz)).astype(BF16)
                return c

            lax.fori_loop(0, h1, body, 0)


def _hyena_order(z, z_off, gate, gate_off, skip, kf, kf_off, tables, n1):
    g1_sig, _, g3, f2, f2i = tables
    b, s, _ = z.shape
    width = skip.shape[1]
    nslab = width // LANES
    it1 = 16
    itm = min(16, n1)
    s1 = LANES // it1
    sm = n1 // itm
    steps = 2 * s1 + sm
    kern = functools.partial(_hyena_kernel, n1=n1, it1=it1, itm=itm, s1=s1, sm=sm)
    one = pl.Buffered(1)
    return pl.pallas_call(
        kern,
        grid=((b // 2) * nslab, steps),
        in_specs=[pl.BlockSpec((2, s, LANES), lambda g, t: (g // nslab, 0, z_off + g % nslab), pipeline_mode=one),
                  pl.BlockSpec((2, s, LANES), lambda g, t: (g // nslab, 0, gate_off + g % nslab),
                               pipeline_mode=one),
                  pl.BlockSpec((1, LANES), lambda g, t: (0, g % nslab)),
                  pl.BlockSpec((it1, 2 * n1, n1), lambda g, t: (jnp.minimum(t, s1 - 1), 0, 0)),
                  pl.BlockSpec((1, itm, 2 * LANES, LANES),
                               lambda g, t: (kf_off + g % nslab, jnp.clip(t - s1, 0, sm - 1), 0, 0)),
                  pl.BlockSpec((it1, n1, 2 * n1), lambda g, t: (jnp.clip(t - s1 - sm, 0, s1 - 1), 0, 0)),
                  pl.BlockSpec((2 * LANES, 2 * LANES), lambda g, t: (0, 0)),
                  pl.BlockSpec((2 * LANES, 2 * LANES), lambda g, t: (0, 0))],
        out_specs=pl.BlockSpec((2, s, LANES), lambda g, t: (g // nslab, 0, g % nslab)),
        out_shape=jax.ShapeDtypeStruct((b, s, width), BF16),
        scratch_shapes=[pltpu.VMEM((n1 * Z_PITCH, LANES), F32),
                        pltpu.VMEM((LANES * _a_pitch(n1), LANES), F32)],
        compiler_params=_cparams(("parallel", "arbitrary")),
        name="hyena",
    )(z, gate, skip, g1_sig, kf, g3, f2, f2i)


def _outproj_kernel(a_ref, z_ref, x_ref, mod_ref, hn_ref, w_ref, o_ref):
    aw = ATTN_WIDTH
    zn = _rms(z_ref[0].astype(F32), hn_ref[...]).astype(BF16)
    y = (jnp.dot(a_ref[0], w_ref[:aw, :], preferred_element_type=F32)
         + jnp.dot(zn, w_ref[aw:, :], preferred_element_type=F32))
    o_ref[0] = x_ref[0] + mod_ref[0][2:3] * y


def _outproj(attn, hz, x, modr, hy_norm, w_out_bf):
    b, s, d = x.shape
    ts = min(512, s)
    aw = attn.shape[2]
    hw = hz.shape[2]
    return pl.pallas_call(
        _outproj_kernel,
        grid=(b, s // ts),
        in_specs=[pl.BlockSpec((1, ts, aw), lambda i, j: (i, j, 0)),
                  pl.BlockSpec((1, ts, hw), lambda i, j: (i, j, 0)),
                  pl.BlockSpec((1, ts, d), lambda i, j: (i, j, 0)),
                  pl.BlockSpec((1, 8, d), lambda i, j: (i, 0, 0)),
                  pl.BlockSpec((1, hw), lambda i, j: (0, 0)),
                  pl.BlockSpec(w_out_bf.shape, lambda i, j: (0, 0))],
        out_specs=pl.BlockSpec((1, ts, d), lambda i, j: (i, j, 0)),
        out_shape=jax.ShapeDtypeStruct((b, s, d), F32),
        compiler_params=_cparams(("parallel", "parallel")),
        name="outproj",
    )(attn, hz, x, modr, hy_norm, w_out_bf)


def _ffn_kernel(x_ref, xp_ref, xn_ref, mod_ref, g_ref, wu_ref, cw_ref, cb_ref, wd_ref, fg_ref, o_ref,
                *, dff, chunks):
    xm = x_ref[0]
    ts = xm.shape[0]
    m = mod_ref[0]
    xa = jnp.concatenate([xp_ref[0], xm, xn_ref[0]], axis=0)
    h = (_rms(xa, g_ref[...]) * (1.0 + m[4:5]) + m[3:4]).astype(BF16)
    hm = h[SUBLANES:SUBLANES + ts]
    masks = _edge_masks(ts)
    acc = jnp.zeros((ts, xm.shape[1]), F32)
    for c0, cf in chunks:
        a = jnp.dot(h, wu_ref[:, c0:c0 + cf], preferred_element_type=F32)
        gate = jnp.dot(hm, wu_ref[:, dff + c0:dff + c0 + cf], preferred_element_type=F32)
        ac = _conv3(a, cw_ref[:, c0:c0 + cf], cb_ref[:, c0:c0 + cf], ts, *masks)
        gl = 0.5 * ac * (1.0 + lax.erf(ac * (2.0 ** -0.5)))
        acc = acc + jnp.dot((gl * gate).astype(BF16), wd_ref[c0:c0 + cf, :], preferred_element_type=F32)
    y = xm + m[5:6] * acc
    o_ref[0] = _rms(y, fg_ref[...])


def _mxu_chunks(n, parts):
    tiles = n // MXU_DIM
    if tiles < parts or n % MXU_DIM:
        return ((0, n),)
    sizes = [(tiles // parts + (1 if i < tiles % parts else 0)) * MXU_DIM for i in range(parts)]
    return tuple((sum(sizes[:i]), sizes[i]) for i in range(parts))


def _ffn(x1, modr, norm_g, wu_bf, conv_w, conv_b, wd_bf, final_g):
    b, s, d = x1.shape
    dff = wd_bf.shape[0]
    ts = min(512, s)
    kern = functools.partial(_ffn_kernel, dff=dff, chunks=_mxu_chunks(dff, 2))
    one = pl.Buffered(1)
    return pl.pallas_call(
        kern,
        grid=(b, s // ts),
        in_specs=_halo_specs(ts, s, d) + [
                  pl.BlockSpec((1, 8, d), lambda i, j: (i, 0, 0)),
                  pl.BlockSpec((1, d), lambda i, j: (0, 0)),
                  pl.BlockSpec(wu_bf.shape, lambda i, j: (0, 0), pipeline_mode=one),
                  pl.BlockSpec(conv_w.shape, lambda i, j: (0, 0)),
                  pl.BlockSpec((1, dff), lambda i, j: (0, 0)),
                  pl.BlockSpec(wd_bf.shape, lambda i, j: (0, 0), pipeline_mode=one),
                  pl.BlockSpec((1, d), lambda i, j: (0, 0))],
        out_specs=pl.BlockSpec((1, ts, d), lambda i, j: (i, j, 0)),
        out_shape=jax.ShapeDtypeStruct((b, s, d), F32),
        compiler_params=_cparams(("parallel", "parallel")),
        name="ffn",
    )(x1, x1, x1, modr, norm_g, wu_bf, conv_w, conv_b.reshape(1, dff), wd_bf, final_g)


def _rope_tables(s):
    nf = DIFF_DH // 4
    inv = ROPE_BASE ** (-jnp.arange(nf, dtype=F32) / nf)
    t = jnp.arange(s, dtype=jnp.int32)
    row = (t // GRID_W).astype(F32)[:, None] * inv
    col = (t % GRID_W).astype(F32)[:, None] * inv
    cos = jnp.concatenate([jnp.cos(row), jnp.cos(row), jnp.cos(col), jnp.cos(col)], axis=1)
    sin = jnp.concatenate([-jnp.sin(row), jnp.sin(row), -jnp.sin(col), jnp.sin(col)], axis=1)
    return jnp.tile(cos, (1, LANES // DIFF_DH)), jnp.tile(sin, (1, LANES // DIFF_DH))


def _filter_features(l):
    pos = jnp.concatenate([jnp.arange(l, dtype=jnp.int32), l - jnp.arange(l, dtype=jnp.int32)])
    valid = jnp.ones((2 * l,), F32).at[l].set(0.0)
    pos = jnp.where(pos == l, 0, pos).astype(F32)[:, None]
    tt = pos / (l - 1)
    w = 2.0 * math.pi * pos / l
    f = jnp.linspace(1e-4, HY_BANDS - 1, HY_BANDS, dtype=F32)[None, :]
    feat = jnp.concatenate([tt, jnp.cos(f * w), -jnp.sin(f * w), valid[:, None]], axis=-1)
    return jnp.pad(feat, ((0, 0), (0, LANES - feat.shape[1])))


def kernel(x, c, ctx, c_ctx, w_mod, b_mod, norm_mix, norm_ffn, w_in, lam_q1, lam_k1, lam_q2, lam_k2, subln, hy_conv_w, hy_conv_b, hy_w1, hy_b1, hy_w2, hy_b2, hy_w3, hy_b3, hy_w4, hy_freq, hy_skip, hy_norm, w_out, ffn_w_up, ffn_conv_w, ffn_conv_b, ffn_w_down, final_norm):
    b, s, d = x.shape
    assert w_mod.shape[0] == 1 and b % 2 == 0 and (2 * s) % (LANES * 16) == 0
    aw = ATTN_WIDTH
    hw = hy_skip.shape[2]
    n1 = 2 * s // LANES

    rows = ((b + 1 + 7) // 8) * 8
    cvec = jnp.zeros((rows, d), F32).at[:b].set(c).at[b].set(c_ctx)
    mod = _modulation(cvec, w_mod[0], b_mod[0])[:b + 1]
    modr = jnp.pad(mod.reshape(b + 1, N_MOD, d), ((0, 0), (0, 8 - N_MOD), (0, 0)))

    w_in_bf = w_in[0].astype(BF16)
    cos_t, sin_t = _rope_tables(s)
    g_mix = norm_mix[0].reshape(1, d)
    w_vt_bf = w_in_bf[:, 2 * aw:3 * aw].T
    q, k_x, vt_x, u = _inproj(x, modr, g_mix, w_in_bf, w_vt_bf, cos_t, sin_t, hy_conv_w[0], hy_conv_b[0])
    k_c, vt_c = _ctxproj(ctx, modr, g_mix, w_in_bf[:, aw:2 * aw], w_vt_bf)
    k_all = jnp.concatenate([k_c, k_x], axis=1)
    vt_all = jnp.concatenate([vt_c, vt_x], axis=2)
    lamp = jnp.concatenate([lam_q1, lam_k1, lam_q2, lam_k2], axis=0)
    attn = _attention(q, k_all, vt_all, lamp, subln[0].reshape(1, DIFF_VD))

    p = {"hy_w1": hy_w1[0], "hy_b1": hy_b1[0], "hy_w2": hy_w2[0], "hy_b2": hy_b2[0],
         "hy_w3": hy_w3[0], "hy_b3": hy_b3[0], "hy_freq": hy_freq[0]}
    fo = hy_w4.shape[1]
    w4r = hy_w4[0].reshape(fo, HY_ORDER, 2, hw).transpose(2, 0, 1, 3).reshape(2, fo, HY_ORDER * hw)
    min_decay = math.log(HY_DECAY_TARGET) / HY_FAST_DECAY_PCT
    max_decay = math.log(HY_DECAY_TARGET) / HY_SLOW_DECAY_PCT
    deltas = jnp.abs(jnp.linspace(min_decay, max_decay, HY_ORDER * 2 * hw, dtype=F32))
    deltas = deltas.reshape(HY_ORDER, 2, hw).transpose(1, 0, 2).reshape(2, 1, HY_ORDER * hw)
    kfull = _hyena_filters(_filter_features(s), p, w4r, deltas)
    tables = _dft_tables(n1)
    kf = _filter_spectra(kfull, tables[1], tables[3], n1)

    nslab = hw // LANES
    z = _hyena_order(u, 0, u, nslab, hy_skip[0, 0:1], kf, 0, tables, n1)
    z = _hyena_order(z, 0, u, 2 * nslab, hy_skip[0, 1:2], kf, nslab, tables, n1)

    x1 = _outproj(attn, z, x, modr, hy_norm[0].reshape(1, hw), w_out[0].astype(BF16))
    return _ffn(x1, modr, norm_ffn[0].reshape(1, d), ffn_w_up[0].astype(BF16), ffn_conv_w[0],
                ffn_conv_b[0], ffn_w_down[0].astype(BF16), final_norm.reshape(1, d))
```

```python
import functools
import math

import jax
import jax.numpy as jnp
from jax import lax
from jax.experimental import pallas as pl
from jax.experimental.pallas import tpu as pltpu

F32 = jnp.float32
BF16 = jnp.bfloat16
HIGHEST = lax.Precision.HIGHEST

EPS = 1e-6
N_MOD = 6
GRID_W = 64
DIFF_HEADS = 4
DIFF_DH = 64
DIFF_VD = 128
ATTN_WIDTH = DIFF_HEADS * DIFF_VD
HY_ORDER = 2
HY_BANDS = 16
HY_DECAY_TARGET = 1e-2
HY_FAST_DECAY_PCT = 0.3
HY_SLOW_DECAY_PCT = 1.5
ROPE_BASE = 10000.0
LAM_INIT = 0.8 - 0.6 * math.exp(-0.3 * 0)

LANES = 128
SUBLANES = 8
MXU_DIM = 256
FFN_HALO = 16
Z_PITCH = LANES + SUBLANES
DFT_STEP_ITERS = 32
DFT_UNROLL = 8
VMEM_LIMIT = 56 * 1024 * 1024
NEG_BIG = -1e30
NT_DIMS = (((1,), (1,)), ((), ()))
ONES_ROWS = 16
Q_SCALE = DIFF_DH ** -0.5 * math.log2(math.e)


def _cparams(sem):
    return pltpu.CompilerParams(dimension_semantics=sem, vmem_limit_bytes=VMEM_LIMIT)


def _rms(x, g):
    return x * lax.rsqrt(jnp.mean(x * x, axis=-1, keepdims=True) + EPS) * g


def _mod_kernel(c_ref, w_ref, b_ref, o_ref):
    c = c_ref[...]
    s = c * jax.nn.sigmoid(c)
    o_ref[...] = jnp.dot(s, w_ref[...], preferred_element_type=F32, precision=HIGHEST) + b_ref[...]


def _modulation(cvec, w_mod, b_mod):
    rows, d = cvec.shape
    cols = w_mod.shape[1]
    bc = 1024
    return pl.pallas_call(
        _mod_kernel,
        grid=(cols // bc,),
        in_specs=[pl.BlockSpec((rows, d), lambda j: (0, 0)),
                  pl.BlockSpec((d, bc), lambda j: (0, j)),
                  pl.BlockSpec((1, bc), lambda j: (0, j))],
        out_specs=pl.BlockSpec((rows, bc), lambda j: (0, j)),
        out_shape=jax.ShapeDtypeStruct((rows, cols), F32),
        compiler_params=_cparams(("arbitrary",)),
        name="mod",
    )(cvec, w_mod, b_mod.reshape(1, cols))


def _rope(t, cos, sin_signed, first_half):
    sw = jnp.where(first_half, pltpu.roll(t, LANES - 16, 1), pltpu.roll(t, 16, 1))
    return t * cos + sw * sin_signed


def _halo_specs(ts, s, d, halo):
    nh = ts // halo
    nblk = s // halo
    return [pl.BlockSpec((1, ts, d), lambda i, j: (i, j, 0)),
            pl.BlockSpec((1, halo, d), lambda i, j: (i, jnp.maximum(j * nh - 1, 0), 0)),
            pl.BlockSpec((1, halo, d), lambda i, j: (i, jnp.minimum((j + 1) * nh, nblk - 1), 0))]


def _with_halo(ref, prev_ref, next_ref):
    return jnp.concatenate([prev_ref[0], ref[0], next_ref[0]], axis=0)


def _conv3(a, w, bias, ts, halo, no_prev, no_next):
    rows = ts + 2 * halo
    prev = jnp.where(no_prev, 0.0, pltpu.roll(a, 1, 0)[halo:halo + ts])
    nxt = jnp.where(no_next, 0.0, pltpu.roll(a, rows - 1, 0)[halo:halo + ts])
    return prev * w[0:1] + a[halo:halo + ts] * w[1:2] + nxt * w[2:3] + bias


def _edge_masks(ts):
    j = pl.program_id(1)
    row = lax.broadcasted_iota(jnp.int32, (ts, 1), 0)
    no_prev = jnp.logical_and(j == 0, row == 0)
    no_next = jnp.logical_and(j == pl.num_programs(1) - 1, row == ts - 1)
    return no_prev, no_next


def _inproj_kernel(x_ref, xp_ref, xn_ref, mod_ref, g_ref, w_ref, wvt_ref, cos_ref, sin_ref, cw_ref, cb_ref,
                   q_ref, k_ref, vt_ref, u_ref):
    ts = x_ref.shape[1]
    m = mod_ref[0]
    xa = _with_halo(x_ref, xp_ref, xn_ref)
    ha = (_rms(xa, g_ref[...]) * (1.0 + m[1:2]) + m[0:1]).astype(BF16)
    h = ha[SUBLANES:SUBLANES + ts]
    cos = cos_ref[...]
    sin = sin_ref[...]
    lane = lax.broadcasted_iota(jnp.int32, cos.shape, 1)
    first_half = (lane % 32) < 16
    aw = ATTN_WIDTH
    q = jnp.dot(h, w_ref[:, :aw], preferred_element_type=F32)
    k = jnp.dot(h, w_ref[:, aw:2 * aw], preferred_element_type=F32)
    for s in range(aw // LANES):
        c0 = s * LANES
        q_ref[0, :, c0:c0 + LANES] = (_rope(q[:, c0:c0 + LANES], cos, sin, first_half) * Q_SCALE).astype(BF16)
        k_ref[0, :, c0:c0 + LANES] = _rope(k[:, c0:c0 + LANES], cos, sin, first_half).astype(BF16)
    vt_ref[0] = lax.dot_general(wvt_ref[...], h, NT_DIMS, preferred_element_type=F32).astype(BF16)
    hy = jnp.dot(ha, w_ref[:, 3 * aw:], preferred_element_type=F32)
    u_ref[0] = _conv3(hy, cw_ref[...], cb_ref[...], ts, SUBLANES, *_edge_masks(ts)).astype(BF16)


def _inproj(x, modr, norm_g, w_in_bf, w_vt_bf, cos_t, sin_t, conv_w, conv_b, ctx_len):
    b, s, d = x.shape
    ts = min(512, s)
    aw = ATTN_WIDTH
    hyc = w_in_bf.shape[1] - 3 * aw
    sk = s + ctx_len
    return pl.pallas_call(
        _inproj_kernel,
        grid=(b, s // ts),
        in_specs=_halo_specs(ts, s, d, SUBLANES) + [
                  pl.BlockSpec((1, 8, d), lambda i, j: (i, 0, 0)),
                  pl.BlockSpec((1, d), lambda i, j: (0, 0)),
                  pl.BlockSpec(w_in_bf.shape, lambda i, j: (0, 0)),
                  pl.BlockSpec(w_vt_bf.shape, lambda i, j: (0, 0)),
                  pl.BlockSpec((ts, LANES), lambda i, j: (j, 0)),
                  pl.BlockSpec((ts, LANES), lambda i, j: (j, 0)),
                  pl.BlockSpec(conv_w.shape, lambda i, j: (0, 0)),
                  pl.BlockSpec((1, hyc), lambda i, j: (0, 0))],
        out_specs=[pl.BlockSpec((1, ts, aw), lambda i, j: (i, j, 0)),
                   pl.BlockSpec((1, ts, aw), lambda i, j: (i, j, 0)),
                   pl.BlockSpec((1, aw, ts), lambda i, j: (i, 0, j)),
                   pl.BlockSpec((1, ts, hyc), lambda i, j: (i, j, 0))],
        out_shape=[jax.ShapeDtypeStruct((b, s, aw), BF16), jax.ShapeDtypeStruct((b, sk, aw), BF16),
                   jax.ShapeDtypeStruct((b, aw, sk), BF16), jax.ShapeDtypeStruct((b, s, hyc), BF16)],
        compiler_params=_cparams(("parallel", "parallel")),
        name="inproj",
    )(x, x, x, modr, norm_g, w_in_bf, w_vt_bf, cos_t, sin_t, conv_w, conv_b.reshape(1, hyc))


def _ctxproj_kernel(x_ref, mod_ref, g_ref, wk_ref, wvt_ref, k_in_ref, vt_in_ref, k_ref, vt_ref):
    del k_in_ref, vt_in_ref
    m = mod_ref[0]
    h = (_rms(x_ref[0], g_ref[...]) * (1.0 + m[1:2]) + m[0:1]).astype(BF16)
    k_ref[0] = jnp.dot(h, wk_ref[...], preferred_element_type=F32).astype(BF16)
    vt_ref[0] = lax.dot_general(wvt_ref[...], h, NT_DIMS, preferred_element_type=F32).astype(BF16)


def _ctxproj(ctx, modr, norm_g, w_k_bf, w_vt_bf, k_all, vt_all):
    b, c, d = ctx.shape
    aw = ATTN_WIDTH
    nb = modr.shape[0] - 1
    blk = (k_all.shape[1] - c) // c
    return pl.pallas_call(
        _ctxproj_kernel,
        grid=(b,),
        in_specs=[pl.BlockSpec((1, c, d), lambda i: (i, 0, 0)),
                  pl.BlockSpec((1, 8, d), lambda i: (nb, 0, 0)),
                  pl.BlockSpec((1, d), lambda i: (0, 0)),
                  pl.BlockSpec(w_k_bf.shape, lambda i: (0, 0)),
                  pl.BlockSpec(w_vt_bf.shape, lambda i: (0, 0)),
                  pl.BlockSpec(memory_space=pl.ANY),
                  pl.BlockSpec(memory_space=pl.ANY)],
        out_specs=[pl.BlockSpec((1, c, aw), lambda i: (i, blk, 0)), pl.BlockSpec((1, aw, c), lambda i: (i, 0, blk))],
        out_shape=[jax.ShapeDtypeStruct(k_all.shape, BF16), jax.ShapeDtypeStruct(vt_all.shape, BF16)],
        input_output_aliases={5: 0, 6: 1},
        compiler_params=_cparams(("parallel",)),
        name="ctxproj",
    )(ctx, modr, norm_g, w_k_bf, w_vt_bf, k_all, vt_all)


def _attn_kernel(q_ref, k_ref, vt_ref, lam_ref, sub_ref, o_ref, qm_scr, sa_scr, sb_scr, acc_scr,
                 *, tq, tk, nq, nkv):
    q = q_ref[0]
    lane = lax.broadcasted_iota(jnp.int32, q.shape, 1)
    zero = jnp.zeros_like(q)
    qm_scr[0] = jnp.where(lane < DIFF_DH, q, zero)
    qm_scr[1] = jnp.where(lane >= DIFF_DH, q, zero)
    acc_scr[...] = jnp.zeros(acc_scr.shape, F32)
    ones = jnp.ones((ONES_ROWS, tk), BF16)
    steps = nq * nkv
    neg = jnp.full((1, tq), NEG_BIG, F32)
    lp = lam_ref[...]
    lam = (jnp.exp(jnp.sum(lp[0:1] * lp[1:2], axis=-1, keepdims=True))
           - jnp.exp(jnp.sum(lp[2:3] * lp[3:4], axis=-1, keepdims=True)) + LAM_INIT)

    def restart(t, m):
        return [jnp.where(t % nkv == 0, neg, mm) for mm in m]

    def scores(t, s_scr, m_cur):
        t = jnp.minimum(t, steps - 1)
        q0 = pl.multiple_of((t // nkv) * tq, tq)
        kj = k_ref[0, pl.ds(pl.multiple_of((t % nkv) * tk, tk), tk), :]
        m_base = restart(t, m_cur)
        m_next = []
        for mp in range(2):
            s = lax.dot_general(kj, qm_scr[mp, pl.ds(q0, tq), :], NT_DIMS, preferred_element_type=F32)
            s_scr[mp] = s
            m_next.append(jnp.maximum(m_base[mp], jnp.max(s, axis=0, keepdims=True)))
        return m_next

    def accumulate(t, s_scr, m_prev, m_cur):
        j = t % nkv
        vt = vt_ref[0, :, pl.ds(pl.multiple_of(j * tk, tk), tk)]
        vext = jnp.concatenate([vt, ones], axis=0)
        m_old = restart(t, m_prev)
        for mp in range(2):
            p = jnp.exp2(s_scr[mp] - m_cur[mp]).astype(BF16)
            alpha = jnp.exp2(m_old[mp] - m_cur[mp])
            acc_scr[mp] = alpha * acc_scr[mp] + jnp.dot(vext, p, preferred_element_type=F32)

        @pl.when(j == nkv - 1)
        def _():
            a0 = acc_scr[0]
            a1 = acc_scr[1]
            vd = DIFF_VD
            o = a0[:vd] / a0[vd:vd + 1] - lam * (a1[:vd] / a1[vd:vd + 1])
            on = o * lax.rsqrt(jnp.mean(o * o, axis=0, keepdims=True) + EPS)
            q0 = pl.multiple_of((t // nkv) * tq, tq)
            o_ref[0, pl.ds(q0, tq), :] = (on.T * sub_ref[...] * (1.0 - LAM_INIT)).astype(BF16)

    m_first = scores(0, sa_scr, [neg, neg])

    def pair(i, carry):
        m_prev, m_cur = list(carry[:2]), list(carry[2:])
        t = 2 * i
        m_nxt = scores(t + 1, sb_scr, m_cur)
        accumulate(t, sa_scr, m_prev, m_cur)
        m_nn = scores(t + 2, sa_scr, m_nxt)
        accumulate(t + 1, sb_scr, m_cur, m_nxt)
        return (*m_nxt, *m_nn)

    carry = lax.fori_loop(0, steps // 2, pair, (neg, neg, *m_first))
    if steps % 2:
        accumulate(steps - 1, sa_scr, list(carry[:2]), list(carry[2:]))


def _pick_tile(n, cands):
    for c in cands:
        if n % c == 0:
            return c
    raise ValueError(f"no tile for {n}")


def _attention(q, k_all, vt_all, lamp, subln):
    b, s, aw = q.shape
    sk = k_all.shape[1]
    tq = min(1024, s)
    tk = _pick_tile(sk, (768, 512, 384, 256, 128))
    kern = functools.partial(_attn_kernel, tq=tq, tk=tk, nq=s // tq, nkv=sk // tk)
    return pl.pallas_call(
        kern,
        grid=(b, DIFF_HEADS),
        in_specs=[pl.BlockSpec((1, s, LANES), lambda i, h: (i, 0, h)),
                  pl.BlockSpec((1, sk, LANES), lambda i, h: (i, 0, h)),
                  pl.BlockSpec((1, DIFF_VD, sk), lambda i, h: (i, h, 0)),
                  pl.BlockSpec(lamp.shape, lambda i, h: (0, 0)),
                  pl.BlockSpec((1, DIFF_VD), lambda i, h: (0, 0))],
        out_specs=pl.BlockSpec((1, s, LANES), lambda i, h: (i, 0, h)),
        out_shape=jax.ShapeDtypeStruct((b, s, aw), BF16),
        scratch_shapes=[pltpu.VMEM((2, s, LANES), BF16),
                        pltpu.VMEM((2, tk, tq), F32),
                        pltpu.VMEM((2, tk, tq), F32),
                        pltpu.VMEM((2, DIFF_VD + ONES_ROWS, tq), F32)],
        compiler_params=_cparams(("parallel", "parallel")),
        name="attn",
    )(q, k_all, vt_all, lamp, subln)


def _filt_kernel(f_ref, w1_ref, b1_ref, w2_ref, b2_ref, w3_ref, b3_ref, fr_ref, w4_ref, dl_ref, o_ref):
    f = f_ref[...]
    fr = fr_ref[...]
    dot = functools.partial(jnp.dot, preferred_element_type=F32, precision=HIGHEST)
    h = jnp.sin(fr * (dot(f, w1_ref[...]) + b1_ref[...]))
    h = jnp.sin(fr * (dot(h, w2_ref[...]) + b2_ref[...]))
    h = jnp.sin(fr * (dot(h, w3_ref[...]) + b3_ref[...]))
    t = f[:, 0:1]
    valid = f[:, 2 * HY_BANDS + 1:2 * HY_BANDS + 2]
    o_ref[...] = dot(h, w4_ref[0]) * jnp.exp(-t * dl_ref[0]) * valid


def _hyena_filters(feat, p, w4r, deltas):
    n = feat.shape[0]
    tr = min(1024, n // 2)
    half_steps = (n // 2) // tr
    fo = p["hy_w2"].shape[0]
    cols = w4r.shape[2]
    const = lambda shape: pl.BlockSpec(shape, lambda i: (0,) * len(shape))
    w1 = jnp.zeros((LANES, fo), F32).at[:p["hy_w1"].shape[0]].set(p["hy_w1"])
    return pl.pallas_call(
        _filt_kernel,
        grid=(n // tr,),
        in_specs=[pl.BlockSpec((tr, LANES), lambda i: (i, 0)),
                  const((LANES, fo)), const((1, fo)), const((fo, fo)), const((1, fo)),
                  const((fo, fo)), const((1, fo)), const((1, fo)),
                  pl.BlockSpec((1, fo, cols), lambda i: (i // half_steps, 0, 0)),
                  pl.BlockSpec((1, 1, cols), lambda i: (i // half_steps, 0, 0))],
        out_specs=pl.BlockSpec((tr, cols), lambda i: (i, 0)),
        out_shape=jax.ShapeDtypeStruct((n, cols), F32),
        compiler_params=_cparams(("parallel",)),
        name="filt",
    )(feat, w1, p["hy_b1"].reshape(1, fo), p["hy_w2"], p["hy_b2"].reshape(1, fo),
      p["hy_w3"], p["hy_b3"].reshape(1, fo), p["hy_freq"].reshape(1, fo), w4r, deltas)


def _dft_tables(n1):
    n2 = LANES
    n = n1 * n2
    h1 = n1 // 2
    i2 = jnp.arange(n2, dtype=jnp.int32)
    i1 = jnp.arange(n1, dtype=jnp.int32)
    at = ((i2[:, None] * i1[None, :]) % n).astype(F32) * (-2.0 * math.pi / n)
    a1 = ((i1[:, None] * i1[None, :]) % n1).astype(F32) * (-2.0 * math.pi / n1)
    tr, ti = jnp.cos(at)[:, :, None], jnp.sin(at)[:, :, None]
    f1r, f1i = jnp.cos(a1)[None], jnp.sin(a1)[None]
    gr, gi = tr * f1r - ti * f1i, tr * f1i + ti * f1r
    grh, gih = gr[:, :, :h1], gi[:, :, :h1]
    g1_sig = jnp.concatenate([jnp.concatenate([grh, -gih], 2), jnp.concatenate([gih, grh], 2)], 1)
    g1_flt = jnp.concatenate([gr, gi], 1)
    qr, qi = jnp.swapaxes(grh, 1, 2), -jnp.swapaxes(gih, 1, 2)
    g3 = jnp.concatenate([jnp.concatenate([qr, -qi], 2), jnp.concatenate([qi, qr], 2)], 1)
    a2 = ((i2[:, None] * i2[None, :]) % n2).astype(F32) * (-2.0 * math.pi / n2)
    fr, fi = jnp.cos(a2), jnp.sin(a2)
    f2 = jnp.concatenate([jnp.concatenate([fr, -fi], 1), jnp.concatenate([fi, fr], 1)], 0)
    f2i = jnp.concatenate([jnp.concatenate([fr, fi], 1), jnp.concatenate([-fi, fr], 1)], 0)
    return (g1_sig.astype(BF16), g1_flt.astype(BF16), g3.astype(BF16), f2.astype(BF16), f2i.astype(BF16))


def _a_pitch(n1):
    return 2 * n1 + SUBLANES


def _stage1(zy_ref, a_ref, g_ref, step, it, n1):
    pa = _a_pitch(n1)

    def body(i, c):
        n2 = step * it + i
        rhs = zy_ref[pl.ds(n2, n1, stride=Z_PITCH), :].astype(BF16)
        a_ref[pl.ds(pl.multiple_of(n2 * pa, SUBLANES), 2 * n1), :] = jnp.dot(
            g_ref[i], rhs, preferred_element_type=F32)
        return c

    lax.fori_loop(0, it, body, 0, unroll=DFT_UNROLL)


def _gather_k1(a_ref, k1, n1):
    pa = _a_pitch(n1)
    re = a_ref[pl.ds(k1, LANES, stride=pa), :]
    im = a_ref[pl.ds(n1 + k1, LANES, stride=pa), :]
    return jnp.concatenate([re, im], axis=0).astype(BF16)


def _load_blocks(dst_ref, src, nblk, base):
    def body(i, c):
        r0 = pl.multiple_of(i * LANES, LANES)
        d0 = pl.multiple_of((base + i) * Z_PITCH, SUBLANES)
        dst_ref[pl.ds(d0, LANES), :] = src(r0).astype(F32)
        return c

    lax.fori_loop(0, nblk, body, 0)


def _fdft_kernel(k_ref, g1_ref, f2_ref, o_ref, zy_ref, a_ref, *, n1, it1, itm, s1):
    t = pl.program_id(1)

    @pl.when(t == 0)
    def _():
        _load_blocks(zy_ref, lambda r0: k_ref[pl.ds(r0, LANES), :], n1, 0)

    @pl.when(t < s1)
    def _():
        _stage1(zy_ref, a_ref, g1_ref, t, it1, n1)

    @pl.when(t >= s1)
    def _():
        f2 = f2_ref[...]
        scale = 1.0 / (n1 * LANES)

        def body(i, c):
            k1 = (t - s1) * itm + i
            y = jnp.dot(f2, _gather_k1(a_ref, k1, n1), preferred_element_type=F32)
            o_ref[0, i] = (y * scale).astype(BF16)
            return c

        lax.fori_loop(0, itm, body, 0, unroll=DFT_UNROLL)


def _filter_spectra(kfull, g1_flt, f2, n1):
    n, cols = kfull.shape
    nslab = cols // LANES
    it1 = DFT_STEP_ITERS
    itm = min(DFT_STEP_ITERS, n1)
    s1 = LANES // it1
    sm = n1 // itm
    kern = functools.partial(_fdft_kernel, n1=n1, it1=it1, itm=itm, s1=s1)
    return pl.pallas_call(
        kern,
        grid=(nslab, s1 + sm),
        in_specs=[pl.BlockSpec((n, LANES), lambda s, t: (0, s)),
                  pl.BlockSpec((it1, 2 * n1, n1), lambda s, t: (jnp.minimum(t, s1 - 1), 0, 0)),
                  pl.BlockSpec((2 * LANES, 2 * LANES), lambda s, t: (0, 0))],
        out_specs=pl.BlockSpec((1, itm, 2 * LANES, LANES), lambda s, t: (s, jnp.maximum(t - s1, 0), 0, 0)),
        out_shape=jax.ShapeDtypeStruct((nslab, n1, 2 * LANES, LANES), BF16),
        scratch_shapes=[pltpu.VMEM((n1 * Z_PITCH, LANES), F32),
                        pltpu.VMEM((LANES * _a_pitch(n1), LANES), F32)],
        compiler_params=_cparams(("parallel", "arbitrary")),
        name="fdft",
    )(kfull, g1_flt, f2)


def _hyena_kernel(z_ref, gate_ref, skip_ref, g1_ref, kf_ref, g3_ref, f2_ref, f2i_ref, o_ref,
                  zy_ref, a_ref, *, n1, it1, itm, s1, sm):
    t = pl.program_id(1)
    h1 = n1 // 2
    pa = _a_pitch(n1)

    @pl.when(t == 0)
    def _():
        for bb in range(2):
            _load_blocks(zy_ref, lambda r0, bb=bb: z_ref[bb, pl.ds(r0, LANES), :], h1, bb * h1)

    @pl.when(t < s1)
    def _():
        _stage1(zy_ref, a_ref, g1_ref, t, it1, n1)

    @pl.when(jnp.logical_and(t >= s1, t < s1 + sm))
    def _():
        f2 = f2_ref[...]
        f2i = f2i_ref[...]

        def body(i, c):
            k1 = (t - s1) * itm + i
            y = jnp.dot(f2, _gather_k1(a_ref, k1, n1), preferred_element_type=F32)
            kf = kf_ref[0, i].astype(F32)
            yr, yi = y[:LANES], y[LANES:]
            kr, ki = kf[:LANES], kf[LANES:]
            prod = jnp.concatenate([yr * kr - yi * ki, yr * ki + yi * kr], axis=0).astype(BF16)
            r = jnp.dot(f2i, prod, preferred_element_type=F32)
            a_ref[pl.ds(k1, LANES, stride=pa), :] = r[:LANES]
            a_ref[pl.ds(n1 + k1, LANES, stride=pa), :] = r[LANES:]
            return c

        lax.fori_loop(0, itm, body, 0, unroll=DFT_UNROLL)

    @pl.when(t >= s1 + sm)
    def _():
        def body(i, c):
            n2 = (t - s1 - sm) * it1 + i
            rhs = a_ref[pl.ds(pl.multiple_of(n2 * pa, SUBLANES), 2 * n1), :].astype(BF16)
            zy_ref[pl.ds(n2, n1, stride=Z_PITCH), :] = jnp.dot(g3_ref[i], rhs, preferred_element_type=F32)
            return c

        lax.fori_loop(0, it1, body, 0, unroll=DFT_UNROLL)

    @pl.when(t == s1 + sm + s1 - 1)
    def _():
        skip = skip_ref[...]
        for bb in range(2):
            def body(i, c, bb=bb):
                r0 = pl.multiple_of(i * LANES, LANES)
                d0 = pl.multiple_of((bb * h1 + i) * Z_PITCH, SUBLANES)
                y = zy_ref[pl.ds(d0, LANES), :]
                zz = z_ref[bb, pl.ds(r0, LANES), :].astype(F32)
                gg = gate_ref[bb, pl.ds(r0, LANES), :].astype(F32)
                o_ref[bb, pl.ds(r0, LANES), :] = (gg * (y + skip * zz)).astype(BF16)
                return c

            lax.fori_loop(0, h1, body, 0)


def _hyena_order(z, z_off, gate, gate_off, skip, kf, kf_off, tables, n1):
    g1_sig, _, g3, f2, f2i = tables
    b, s, _ = z.shape
    width = skip.shape[1]
    nslab = width // LANES
    it1 = DFT_STEP_ITERS
    itm = min(DFT_STEP_ITERS, n1)
    s1 = LANES // it1
    sm = n1 // itm
    steps = 2 * s1 + sm
    kern = functools.partial(_hyena_kernel, n1=n1, it1=it1, itm=itm, s1=s1, sm=sm)
    one = pl.Buffered(1)
    return pl.pallas_call(
        kern,
        grid=((b // 2) * nslab, steps),
        in_specs=[pl.BlockSpec((2, s, LANES), lambda g, t: (g // nslab, 0, z_off + g % nslab), pipeline_mode=one),
                  pl.BlockSpec((2, s, LANES), lambda g, t: (g // nslab, 0, gate_off + g % nslab),
                               pipeline_mode=one),
                  pl.BlockSpec((1, LANES), lambda g, t: (0, g % nslab)),
                  pl.BlockSpec((it1, 2 * n1, n1), lambda g, t: (jnp.minimum(t, s1 - 1), 0, 0)),
                  pl.BlockSpec((1, itm, 2 * LANES, LANES),
                               lambda g, t: (kf_off + g % nslab, jnp.clip(t - s1, 0, sm - 1), 0, 0)),
                  pl.BlockSpec((it1, n1, 2 * n1), lambda g, t: (jnp.clip(t - s1 - sm, 0, s1 - 1), 0, 0)),
                  pl.BlockSpec((2 * LANES, 2 * LANES), lambda g, t: (0, 0)),
                  pl.BlockSpec((2 * LANES, 2 * LANES), lambda g, t: (0, 0))],
        out_specs=pl.BlockSpec((2, s, LANES), lambda g, t: (g // nslab, 0, g % nslab)),
        out_shape=jax.ShapeDtypeStruct((b, s, width), BF16),
        scratch_shapes=[pltpu.VMEM((n1 * Z_PITCH, LANES), F32),
                        pltpu.VMEM((LANES * _a_pitch(n1), LANES), F32)],
        compiler_params=_cparams(("parallel", "arbitrary")),
        name="hyena",
    )(z, gate, skip, g1_sig, kf, g3, f2, f2i)


def _ffn_kernel(x_ref, xp_ref, xn_ref, a_ref, ap_ref, an_ref, z_ref, zp_ref, zn_ref, mod_ref, hn_ref, wo_ref,
                g_ref, wu_ref, cw_ref, cb_ref, wd_ref, fg_ref, o_ref, *, dff, chunks):
    ts = x_ref.shape[1]
    m = mod_ref[0]
    aw = ATTN_WIDTH
    zn = _rms(_with_halo(z_ref, zp_ref, zn_ref).astype(F32), hn_ref[...]).astype(BF16)
    mix = (jnp.dot(_with_halo(a_ref, ap_ref, an_ref), wo_ref[:aw, :], preferred_element_type=F32)
           + jnp.dot(zn, wo_ref[aw:, :], preferred_element_type=F32))
    x1 = _with_halo(x_ref, xp_ref, xn_ref) + m[2:3] * mix
    h = (_rms(x1, g_ref[...]) * (1.0 + m[4:5]) + m[3:4]).astype(BF16)
    hm = h[FFN_HALO:FFN_HALO + ts]
    masks = _edge_masks(ts)
    acc = jnp.zeros((ts, x1.shape[1]), F32)
    for c0, cf in chunks:
        a = jnp.dot(h, wu_ref[:, c0:c0 + cf], preferred_element_type=F32)
        gate = jnp.dot(hm, wu_ref[:, dff + c0:dff + c0 + cf], preferred_element_type=F32)
        ac = _conv3(a, cw_ref[:, c0:c0 + cf], cb_ref[:, c0:c0 + cf], ts, FFN_HALO, *masks)
        gl = 0.5 * ac * (1.0 + lax.erf(ac * (2.0 ** -0.5)))
        acc = acc + jnp.dot((gl * gate).astype(BF16), wd_ref[c0:c0 + cf, :], preferred_element_type=F32)
    y = x1[FFN_HALO:FFN_HALO + ts] + m[5:6] * acc
    o_ref[0] = _rms(y, fg_ref[...])


def _mxu_chunks(n, parts):
    tiles = n // MXU_DIM
    if tiles < parts or n % MXU_DIM:
        return ((0, n),)
    sizes = [(tiles // parts + (1 if i < tiles % parts else 0)) * MXU_DIM for i in range(parts)]
    return tuple((sum(sizes[:i]), sizes[i]) for i in range(parts))


def _ffn(x, attn, hz, modr, hy_norm, w_out_bf, norm_g, wu_bf, conv_w, conv_b, wd_bf, final_g):
    b, s, d = x.shape
    dff = wd_bf.shape[0]
    ts = min(512, s)
    kern = functools.partial(_ffn_kernel, dff=dff, chunks=_mxu_chunks(dff, 2))
    one = pl.Buffered(1)
    const = lambda shape, **kw: pl.BlockSpec(shape, lambda i, j: (0,) * len(shape), **kw)
    return pl.pallas_call(
        kern,
        grid=(b, s // ts),
        in_specs=(_halo_specs(ts, s, d, FFN_HALO) + _halo_specs(ts, s, attn.shape[2], FFN_HALO)
                  + _halo_specs(ts, s, hz.shape[2], FFN_HALO) + [
                      pl.BlockSpec((1, 8, d), lambda i, j: (i, 0, 0)),
                      const(hy_norm.shape),
                      const(w_out_bf.shape, pipeline_mode=one),
                      const((1, d)),
                      const(wu_bf.shape, pipeline_mode=one),
                      const(conv_w.shape),
                      const((1, dff)),
                      const(wd_bf.shape, pipeline_mode=one),
                      const((1, d))]),
        out_specs=pl.BlockSpec((1, ts, d), lambda i, j: (i, j, 0)),
        out_shape=jax.ShapeDtypeStruct((b, s, d), F32),
        compiler_params=_cparams(("parallel", "parallel")),
        name="ffn",
    )(x, x, x, attn, attn, attn, hz, hz, hz, modr, hy_norm, w_out_bf, norm_g, wu_bf, conv_w,
      conv_b.reshape(1, dff), wd_bf, final_g)


def _rope_tables(s):
    nf = DIFF_DH // 4
    inv = ROPE_BASE ** (-jnp.arange(nf, dtype=F32) / nf)
    t = jnp.arange(s, dtype=jnp.int32)
    row = (t // GRID_W).astype(F32)[:, None] * inv
    col = (t % GRID_W).astype(F32)[:, None] * inv
    cos = jnp.concatenate([jnp.cos(row), jnp.cos(row), jnp.cos(col), jnp.cos(col)], axis=1)
    sin = jnp.concatenate([-jnp.sin(row), jnp.sin(row), -jnp.sin(col), jnp.sin(col)], axis=1)
    return jnp.tile(cos, (1, LANES // DIFF_DH)), jnp.tile(sin, (1, LANES // DIFF_DH))


def _filter_features(l):
    pos = jnp.concatenate([jnp.arange(l, dtype=jnp.int32), l - jnp.arange(l, dtype=jnp.int32)])
    valid = jnp.ones((2 * l,), F32).at[l].set(0.0)
    pos = jnp.where(pos == l, 0, pos).astype(F32)[:, None]
    tt = pos / (l - 1)
    w = 2.0 * math.pi * pos / l
    f = jnp.linspace(1e-4, HY_BANDS - 1, HY_BANDS, dtype=F32)[None, :]
    feat = jnp.concatenate([tt, jnp.cos(f * w), -jnp.sin(f * w), valid[:, None]], axis=-1)
    return jnp.pad(feat, ((0, 0), (0, LANES - feat.shape[1])))


def kernel(x, c, ctx, c_ctx, w_mod, b_mod, norm_mix, norm_ffn, w_in, lam_q1, lam_k1, lam_q2, lam_k2, subln, hy_conv_w, hy_conv_b, hy_w1, hy_b1, hy_w2, hy_b2, hy_w3, hy_b3, hy_w4, hy_freq, hy_skip, hy_norm, w_out, ffn_w_up, ffn_conv_w, ffn_conv_b, ffn_w_down, final_norm):
    b, s, d = x.shape
    assert w_mod.shape[0] == 1 and b % 2 == 0 and (2 * s) % (LANES * 16) == 0
    assert s % ctx.shape[1] == 0 and ctx.shape[1] % LANES == 0
    aw = ATTN_WIDTH
    hw = hy_skip.shape[2]
    n1 = 2 * s // LANES

    rows = ((b + 1 + 7) // 8) * 8
    cvec = jnp.zeros((rows, d), F32).at[:b].set(c).at[b].set(c_ctx)
    mod = _modulation(cvec, w_mod[0], b_mod[0])[:b + 1]
    modr = jnp.pad(mod.reshape(b + 1, N_MOD, d), ((0, 0), (0, 8 - N_MOD), (0, 0)))

    w_in_bf = w_in[0].astype(BF16)
    cos_t, sin_t = _rope_tables(s)
    g_mix = norm_mix[0].reshape(1, d)
    w_vt_bf = w_in_bf[:, 2 * aw:3 * aw].T
    q, k_all, vt_all, u = _inproj(x, modr, g_mix, w_in_bf, w_vt_bf, cos_t, sin_t, hy_conv_w[0], hy_conv_b[0],
                                  ctx.shape[1])
    k_all, vt_all = _ctxproj(ctx, modr, g_mix, w_in_bf[:, aw:2 * aw], w_vt_bf, k_all, vt_all)
    lamp = jnp.concatenate([lam_q1, lam_k1, lam_q2, lam_k2], axis=0)
    attn = _attention(q, k_all, vt_all, lamp, subln[0].reshape(1, DIFF_VD))

    p = {"hy_w1": hy_w1[0], "hy_b1": hy_b1[0], "hy_w2": hy_w2[0], "hy_b2": hy_b2[0],
         "hy_w3": hy_w3[0], "hy_b3": hy_b3[0], "hy_freq": hy_freq[0]}
    fo = hy_w4.shape[1]
    w4r = hy_w4[0].reshape(fo, HY_ORDER, 2, hw).transpose(2, 0, 1, 3).reshape(2, fo, HY_ORDER * hw)
    min_decay = math.log(HY_DECAY_TARGET) / HY_FAST_DECAY_PCT
    max_decay = math.log(HY_DECAY_TARGET) / HY_SLOW_DECAY_PCT
    deltas = jnp.abs(jnp.linspace(min_decay, max_decay, HY_ORDER * 2 * hw, dtype=F32))
    deltas = deltas.reshape(HY_ORDER, 2, hw).transpose(1, 0, 2).reshape(2, 1, HY_ORDER * hw)
    kfull = _hyena_filters(_filter_features(s), p, w4r, deltas)
    tables = _dft_tables(n1)
    kf = _filter_spectra(kfull, tables[1], tables[3], n1)

    nslab = hw // LANES
    z = _hyena_order(u, 0, u, nslab, hy_skip[0, 0:1], kf, 0, tables, n1)
    z = _hyena_order(z, 0, u, 2 * nslab, hy_skip[0, 1:2], kf, nslab, tables, n1)

    return _ffn(x, attn, z, modr, hy_norm[0].reshape(1, hw), w_out[0].astype(BF16), norm_ffn[0].reshape(1, d),
                ffn_w_up[0].astype(BF16), ffn_conv_w[0], ffn_conv_b[0], ffn_w_down[0].astype(BF16),
                final_norm.reshape(1, d))
```

```python
import functools
import math

import jax
import jax.numpy as jnp
from jax import lax
from jax.experimental import pallas as pl
from jax.experimental.pallas import tpu as pltpu

F32 = jnp.float32
BF16 = jnp.bfloat16
HIGHEST = lax.Precision.HIGHEST

EPS = 1e-6
N_MOD = 6
GRID_W = 64
DIFF_HEADS = 4
DIFF_DH = 64
DIFF_VD = 128
ATTN_WIDTH = DIFF_HEADS * DIFF_VD
HY_ORDER = 2
HY_BANDS = 16
HY_DECAY_TARGET = 1e-2
HY_FAST_DECAY_PCT = 0.3
HY_SLOW_DECAY_PCT = 1.5
ROPE_BASE = 10000.0
LAM_INIT = 0.8 - 0.6 * math.exp(-0.3 * 0)

LANES = 128
SUBLANES = 8
MXU_DIM = 256
FFN_HALO = 16
Z_PITCH = LANES + SUBLANES
DFT_STEP_ITERS = 32
DFT_UNROLL = 8
VMEM_LIMIT = 56 * 1024 * 1024
NEG_BIG = -1e30
NT_DIMS = (((1,), (1,)), ((), ()))
ONES_ROWS = 16
Q_SCALE = DIFF_DH ** -0.5 * math.log2(math.e)


def _cparams(sem):
    return pltpu.CompilerParams(dimension_semantics=sem, vmem_limit_bytes=VMEM_LIMIT)


def _rms(x, g):
    return x * lax.rsqrt(jnp.mean(x * x, axis=-1, keepdims=True) + EPS) * g


def _mod_kernel(c_ref, w_ref, b_ref, o_ref):
    c = c_ref[...]
    s = c * jax.nn.sigmoid(c)
    o_ref[...] = jnp.dot(s, w_ref[...], preferred_element_type=F32, precision=HIGHEST) + b_ref[...]


def _modulation(cvec, w_mod, b_mod):
    rows, d = cvec.shape
    cols = w_mod.shape[1]
    bc = 1024
    return pl.pallas_call(
        _mod_kernel,
        grid=(cols // bc,),
        in_specs=[pl.BlockSpec((rows, d), lambda j: (0, 0)),
                  pl.BlockSpec((d, bc), lambda j: (0, j)),
                  pl.BlockSpec((1, bc), lambda j: (0, j))],
        out_specs=pl.BlockSpec((rows, bc), lambda j: (0, j)),
        out_shape=jax.ShapeDtypeStruct((rows, cols), F32),
        compiler_params=_cparams(("arbitrary",)),
        name="mod",
    )(cvec, w_mod, b_mod.reshape(1, cols))


def _rope(t, cos, sin_signed, first_half):
    sw = jnp.where(first_half, pltpu.roll(t, LANES - 16, 1), pltpu.roll(t, 16, 1))
    return t * cos + sw * sin_signed


def _halo_specs(ts, s, d, halo):
    nh = ts // halo
    nblk = s // halo
    return [pl.BlockSpec((1, ts, d), lambda i, j: (i, j, 0)),
            pl.BlockSpec((1, halo, d), lambda i, j: (i, jnp.maximum(j * nh - 1, 0), 0)),
            pl.BlockSpec((1, halo, d), lambda i, j: (i, jnp.minimum((j + 1) * nh, nblk - 1), 0))]


def _slab_halo_specs(ts, s, nslab, halo):
    nh = ts // halo
    nblk = s // halo
    return [pl.BlockSpec((1, nslab, ts, LANES), lambda i, j: (i, 0, j, 0)),
            pl.BlockSpec((1, nslab, halo, LANES), lambda i, j: (i, 0, jnp.maximum(j * nh - 1, 0), 0)),
            pl.BlockSpec((1, nslab, halo, LANES), lambda i, j: (i, 0, jnp.minimum((j + 1) * nh, nblk - 1), 0))]


def _with_halo(ref, prev_ref, next_ref):
    return jnp.concatenate([prev_ref[0], ref[0], next_ref[0]], axis=0)


def _conv3(a, w, bias, ts, halo, no_prev, no_next):
    rows = ts + 2 * halo
    prev = jnp.where(no_prev, 0.0, pltpu.roll(a, 1, 0)[halo:halo + ts])
    nxt = jnp.where(no_next, 0.0, pltpu.roll(a, rows - 1, 0)[halo:halo + ts])
    return prev * w[0:1] + a[halo:halo + ts] * w[1:2] + nxt * w[2:3] + bias


def _edge_masks(ts):
    j = pl.program_id(1)
    row = lax.broadcasted_iota(jnp.int32, (ts, 1), 0)
    no_prev = jnp.logical_and(j == 0, row == 0)
    no_next = jnp.logical_and(j == pl.num_programs(1) - 1, row == ts - 1)
    return no_prev, no_next


def _inproj_kernel(x_ref, xp_ref, xn_ref, mod_ref, g_ref, w_ref, wvt_ref, cos_ref, sin_ref, cw_ref, cb_ref,
                   q_ref, k_ref, vt_ref, u_ref):
    ts = x_ref.shape[1]
    m = mod_ref[0]
    xa = _with_halo(x_ref, xp_ref, xn_ref)
    ha = (_rms(xa, g_ref[...]) * (1.0 + m[1:2]) + m[0:1]).astype(BF16)
    h = ha[SUBLANES:SUBLANES + ts]
    cos = cos_ref[...]
    sin = sin_ref[...]
    lane = lax.broadcasted_iota(jnp.int32, cos.shape, 1)
    first_half = (lane % 32) < 16
    aw = ATTN_WIDTH
    q = jnp.dot(h, w_ref[:, :aw], preferred_element_type=F32)
    k = jnp.dot(h, w_ref[:, aw:2 * aw], preferred_element_type=F32)
    for s in range(aw // LANES):
        c0 = s * LANES
        q_ref[0, :, c0:c0 + LANES] = (_rope(q[:, c0:c0 + LANES], cos, sin, first_half) * Q_SCALE).astype(BF16)
        k_ref[0, :, c0:c0 + LANES] = _rope(k[:, c0:c0 + LANES], cos, sin, first_half).astype(BF16)
    vt_ref[0] = lax.dot_general(wvt_ref[...], h, NT_DIMS, preferred_element_type=F32).astype(BF16)
    hy = jnp.dot(ha, w_ref[:, 3 * aw:], preferred_element_type=F32)
    u = _conv3(hy, cw_ref[...], cb_ref[...], ts, SUBLANES, *_edge_masks(ts)).astype(BF16)
    for sl in range(u_ref.shape[1]):
        u_ref[0, sl] = u[:, sl * LANES:(sl + 1) * LANES]


def _inproj(x, modr, norm_g, w_in_bf, w_vt_bf, cos_t, sin_t, conv_w, conv_b, ctx_len):
    b, s, d = x.shape
    ts = min(512, s)
    aw = ATTN_WIDTH
    hyc = w_in_bf.shape[1] - 3 * aw
    sk = s + ctx_len
    return pl.pallas_call(
        _inproj_kernel,
        grid=(b, s // ts),
        in_specs=_halo_specs(ts, s, d, SUBLANES) + [
                  pl.BlockSpec((1, 8, d), lambda i, j: (i, 0, 0)),
                  pl.BlockSpec((1, d), lambda i, j: (0, 0)),
                  pl.BlockSpec(w_in_bf.shape, lambda i, j: (0, 0)),
                  pl.BlockSpec(w_vt_bf.shape, lambda i, j: (0, 0)),
                  pl.BlockSpec((ts, LANES), lambda i, j: (j, 0)),
                  pl.BlockSpec((ts, LANES), lambda i, j: (j, 0)),
                  pl.BlockSpec(conv_w.shape, lambda i, j: (0, 0)),
                  pl.BlockSpec((1, hyc), lambda i, j: (0, 0))],
        out_specs=[pl.BlockSpec((1, ts, aw), lambda i, j: (i, j, 0)),
                   pl.BlockSpec((1, ts, aw), lambda i, j: (i, j, 0)),
                   pl.BlockSpec((1, aw, ts), lambda i, j: (i, 0, j)),
                   pl.BlockSpec((1, hyc // LANES, ts, LANES), lambda i, j: (i, 0, j, 0))],
        out_shape=[jax.ShapeDtypeStruct((b, s, aw), BF16), jax.ShapeDtypeStruct((b, sk, aw), BF16),
                   jax.ShapeDtypeStruct((b, aw, sk), BF16),
                   jax.ShapeDtypeStruct((b, hyc // LANES, s, LANES), BF16)],
        compiler_params=_cparams(("parallel", "parallel")),
        name="inproj",
    )(x, x, x, modr, norm_g, w_in_bf, w_vt_bf, cos_t, sin_t, conv_w, conv_b.reshape(1, hyc))


def _ctxproj_kernel(x_ref, mod_ref, g_ref, wk_ref, wvt_ref, k_in_ref, vt_in_ref, k_ref, vt_ref):
    del k_in_ref, vt_in_ref
    m = mod_ref[0]
    h = (_rms(x_ref[0], g_ref[...]) * (1.0 + m[1:2]) + m[0:1]).astype(BF16)
    k_ref[0] = jnp.dot(h, wk_ref[...], preferred_element_type=F32).astype(BF16)
    vt_ref[0] = lax.dot_general(wvt_ref[...], h, NT_DIMS, preferred_element_type=F32).astype(BF16)


def _ctxproj(ctx, modr, norm_g, w_k_bf, w_vt_bf, k_all, vt_all):
    b, c, d = ctx.shape
    aw = ATTN_WIDTH
    nb = modr.shape[0] - 1
    blk = (k_all.shape[1] - c) // c
    return pl.pallas_call(
        _ctxproj_kernel,
        grid=(b,),
        in_specs=[pl.BlockSpec((1, c, d), lambda i: (i, 0, 0)),
                  pl.BlockSpec((1, 8, d), lambda i: (nb, 0, 0)),
                  pl.BlockSpec((1, d), lambda i: (0, 0)),
                  pl.BlockSpec(w_k_bf.shape, lambda i: (0, 0)),
                  pl.BlockSpec(w_vt_bf.shape, lambda i: (0, 0)),
                  pl.BlockSpec(memory_space=pl.ANY),
                  pl.BlockSpec(memory_space=pl.ANY)],
        out_specs=[pl.BlockSpec((1, c, aw), lambda i: (i, blk, 0)), pl.BlockSpec((1, aw, c), lambda i: (i, 0, blk))],
        out_shape=[jax.ShapeDtypeStruct(k_all.shape, BF16), jax.ShapeDtypeStruct(vt_all.shape, BF16)],
        input_output_aliases={5: 0, 6: 1},
        compiler_params=_cparams(("parallel",)),
        name="ctxproj",
    )(ctx, modr, norm_g, w_k_bf, w_vt_bf, k_all, vt_all)


def _attn_kernel(q_ref, k_ref, vt_ref, lam_ref, sub_ref, o_ref, qm_scr, sa_scr, sb_scr, acc_scr,
                 *, tq, tk, nq, nkv):
    q = q_ref[0]
    lane = lax.broadcasted_iota(jnp.int32, q.shape, 1)
    zero = jnp.zeros_like(q)
    qm_scr[0] = jnp.where(lane < DIFF_DH, q, zero)
    qm_scr[1] = jnp.where(lane >= DIFF_DH, q, zero)
    acc_scr[...] = jnp.zeros(acc_scr.shape, F32)
    ones = jnp.ones((ONES_ROWS, tk), BF16)
    steps = nq * nkv
    neg = jnp.full((1, tq), NEG_BIG, F32)
    lp = lam_ref[...]
    lam = (jnp.exp(jnp.sum(lp[0:1] * lp[1:2], axis=-1, keepdims=True))
           - jnp.exp(jnp.sum(lp[2:3] * lp[3:4], axis=-1, keepdims=True)) + LAM_INIT)

    def restart(t, m):
        return [jnp.where(t % nkv == 0, neg, mm) for mm in m]

    def scores(t, s_scr, m_cur):
        t = jnp.minimum(t, steps - 1)
        q0 = pl.multiple_of((t // nkv) * tq, tq)
        kj = k_ref[0, pl.ds(pl.multiple_of((t % nkv) * tk, tk), tk), :]
        m_base = restart(t, m_cur)
        m_next = []
        for mp in range(2):
            s = lax.dot_general(kj, qm_scr[mp, pl.ds(q0, tq), :], NT_DIMS, preferred_element_type=F32)
            s_scr[mp] = s
            m_next.append(jnp.maximum(m_base[mp], jnp.max(s, axis=0, keepdims=True)))
        return m_next

    def accumulate(t, s_scr, m_prev, m_cur):
        j = t % nkv
        vt = vt_ref[0, :, pl.ds(pl.multiple_of(j * tk, tk), tk)]
        vext = jnp.concatenate([vt, ones], axis=0)
        m_old = restart(t, m_prev)
        for mp in range(2):
            p = jnp.exp2(s_scr[mp] - m_cur[mp]).astype(BF16)
            alpha = jnp.exp2(m_old[mp] - m_cur[mp])
            acc_scr[mp] = alpha * acc_scr[mp] + jnp.dot(vext, p, preferred_element_type=F32)

        @pl.when(j == nkv - 1)
        def _():
            a0 = acc_scr[0]
            a1 = acc_scr[1]
            vd = DIFF_VD
            o = a0[:vd] / a0[vd:vd + 1] - lam * (a1[:vd] / a1[vd:vd + 1])
            on = o * lax.rsqrt(jnp.mean(o * o, axis=0, keepdims=True) + EPS)
            q0 = pl.multiple_of((t // nkv) * tq, tq)
            o_ref[0, pl.ds(q0, tq), :] = (on.T * sub_ref[...] * (1.0 - LAM_INIT)).astype(BF16)

    m_first = scores(0, sa_scr, [neg, neg])

    def pair(i, carry):
        m_prev, m_cur = list(carry[:2]), list(carry[2:])
        t = 2 * i
        m_nxt = scores(t + 1, sb_scr, m_cur)
        accumulate(t, sa_scr, m_prev, m_cur)
        m_nn = scores(t + 2, sa_scr, m_nxt)
        accumulate(t + 1, sb_scr, m_cur, m_nxt)
        return (*m_nxt, *m_nn)

    carry = lax.fori_loop(0, steps // 2, pair, (neg, neg, *m_first))
    if steps % 2:
        accumulate(steps - 1, sa_scr, list(carry[:2]), list(carry[2:]))


def _pick_tile(n, cands):
    for c in cands:
        if n % c == 0:
            return c
    raise ValueError(f"no tile for {n}")


def _attention(q, k_all, vt_all, lamp, subln):
    b, s, aw = q.shape
    sk = k_all.shape[1]
    tq = min(1024, s)
    tk = _pick_tile(sk, (1408, 768, 512, 384, 256, 128))
    kern = functools.partial(_attn_kernel, tq=tq, tk=tk, nq=s // tq, nkv=sk // tk)
    return pl.pallas_call(
        kern,
        grid=(b, DIFF_HEADS),
        in_specs=[pl.BlockSpec((1, s, LANES), lambda i, h: (i, 0, h)),
                  pl.BlockSpec((1, sk, LANES), lambda i, h: (i, 0, h)),
                  pl.BlockSpec((1, DIFF_VD, sk), lambda i, h: (i, h, 0)),
                  pl.BlockSpec(lamp.shape, lambda i, h: (0, 0)),
                  pl.BlockSpec((1, DIFF_VD), lambda i, h: (0, 0))],
        out_specs=pl.BlockSpec((1, s, LANES), lambda i, h: (i, 0, h)),
        out_shape=jax.ShapeDtypeStruct((b, s, aw), BF16),
        scratch_shapes=[pltpu.VMEM((2, s, LANES), BF16),
                        pltpu.VMEM((2, tk, tq), F32),
                        pltpu.VMEM((2, tk, tq), F32),
                        pltpu.VMEM((2, DIFF_VD + ONES_ROWS, tq), F32)],
        compiler_params=_cparams(("parallel", "parallel")),
        name="attn",
    )(q, k_all, vt_all, lamp, subln)


def _filt_kernel(ft_ref, tv_ref, w1t_ref, b1_ref, w2t_ref, b2_ref, w3t_ref, b3_ref, fr_ref, w4_ref, dl_ref, o_ref):
    dot = functools.partial(jnp.dot, preferred_element_type=F32, precision=HIGHEST)
    fr = fr_ref[...]
    h = jnp.sin(fr * (dot(w1t_ref[...], ft_ref[...]) + b1_ref[...]))
    h = jnp.sin(fr * (dot(w2t_ref[...], h) + b2_ref[...]))
    h = jnp.sin(fr * (dot(w3t_ref[...], h) + b3_ref[...]))
    k = lax.dot_general(h, w4_ref[0], (((0,), (0,)), ((), ())), preferred_element_type=F32,
                        precision=HIGHEST)
    tv = tv_ref[...]
    o_ref[...] = k * jnp.exp(-tv[:, 0:1] * dl_ref[0]) * tv[:, 1:2]


def _hyena_filters(feat_t, tv, p, w4r, deltas):
    n = feat_t.shape[1]
    tr = min(1024, n // 2)
    half_steps = (n // 2) // tr
    fo = p["hy_w2"].shape[0]
    cols = w4r.shape[2]
    const = lambda shape: pl.BlockSpec(shape, lambda i: (0,) * len(shape))
    w1t = jnp.zeros((fo, LANES), F32).at[:, :p["hy_w1"].shape[0]].set(p["hy_w1"].T)
    col = lambda v: v.reshape(fo, 1)
    return pl.pallas_call(
        _filt_kernel,
        grid=(n // tr,),
        in_specs=[pl.BlockSpec((LANES, tr), lambda i: (0, i)),
                  pl.BlockSpec((tr, tv.shape[1]), lambda i: (i, 0)),
                  const((fo, LANES)), const((fo, 1)), const((fo, fo)), const((fo, 1)),
                  const((fo, fo)), const((fo, 1)), const((fo, 1)),
                  pl.BlockSpec((1, fo, cols), lambda i: (i // half_steps, 0, 0)),
                  pl.BlockSpec((1, 1, cols), lambda i: (i // half_steps, 0, 0))],
        out_specs=pl.BlockSpec((tr, cols), lambda i: (i, 0)),
        out_shape=jax.ShapeDtypeStruct((n, cols), F32),
        compiler_params=_cparams(("parallel",)),
        name="filt",
    )(feat_t, tv, w1t, col(p["hy_b1"]), p["hy_w2"].T, col(p["hy_b2"]), p["hy_w3"].T, col(p["hy_b3"]),
      col(p["hy_freq"]), w4r, deltas)


def _dft_tables(n1):
    n2 = LANES
    n = n1 * n2
    h1 = n1 // 2
    i2 = jnp.arange(n2, dtype=jnp.int32)
    i1 = jnp.arange(n1, dtype=jnp.int32)
    at = ((i2[:, None] * i1[None, :]) % n).astype(F32) * (-2.0 * math.pi / n)
    a1 = ((i1[:, None] * i1[None, :]) % n1).astype(F32) * (-2.0 * math.pi / n1)
    tr, ti = jnp.cos(at)[:, :, None], jnp.sin(at)[:, :, None]
    f1r, f1i = jnp.cos(a1)[None], jnp.sin(a1)[None]
    gr, gi = tr * f1r - ti * f1i, tr * f1i + ti * f1r
    grh, gih = gr[:, :, :h1], gi[:, :, :h1]
    g1_sig = jnp.concatenate([jnp.concatenate([grh, -gih], 2), jnp.concatenate([gih, grh], 2)], 1)
    g1_flt = jnp.concatenate([gr, gi], 1)
    qr, qi = jnp.swapaxes(grh, 1, 2), -jnp.swapaxes(gih, 1, 2)
    g3 = jnp.concatenate([jnp.concatenate([qr, -qi], 2), jnp.concatenate([qi, qr], 2)], 1)
    a2 = ((i2[:, None] * i2[None, :]) % n2).astype(F32) * (-2.0 * math.pi / n2)
    fr, fi = jnp.cos(a2), jnp.sin(a2)
    f2 = jnp.concatenate([jnp.concatenate([fr, -fi], 1), jnp.concatenate([fi, fr], 1)], 0)
    f2i = jnp.concatenate([jnp.concatenate([fr, fi], 1), jnp.concatenate([-fi, fr], 1)], 0)
    return (g1_sig.astype(BF16), g1_flt.astype(BF16), g3.astype(BF16), f2.astype(BF16), f2i.astype(BF16))


def _a_pitch(n1):
    return 2 * n1 + SUBLANES


def _stage1(zy_ref, a_ref, g_ref, step, it, n1):
    pa = _a_pitch(n1)

    def body(i, c):
        n2 = step * it + i
        rhs = zy_ref[pl.ds(n2, n1, stride=Z_PITCH), :].astype(BF16)
        a_ref[pl.ds(pl.multiple_of(n2 * pa, SUBLANES), 2 * n1), :] = jnp.dot(
            g_ref[i], rhs, preferred_element_type=F32)
        return c

    lax.fori_loop(0, it, body, 0, unroll=DFT_UNROLL)


def _gather_k1(a_ref, k1, n1):
    pa = _a_pitch(n1)
    re = a_ref[pl.ds(k1, LANES, stride=pa), :]
    im = a_ref[pl.ds(n1 + k1, LANES, stride=pa), :]
    return jnp.concatenate([re, im], axis=0).astype(BF16)


def _load_blocks(dst_ref, src, nblk, base):
    def body(i, c):
        r0 = pl.multiple_of(i * LANES, LANES)
        d0 = pl.multiple_of((base + i) * Z_PITCH, SUBLANES)
        dst_ref[pl.ds(d0, LANES), :] = src(r0).astype(F32)
        return c

    lax.fori_loop(0, nblk, body, 0)


def _fdft_kernel(k_ref, g1_ref, f2_ref, o_ref, zy_ref, a_ref, *, n1, it1, itm, s1):
    t = pl.program_id(1)

    @pl.when(t == 0)
    def _():
        _load_blocks(zy_ref, lambda r0: k_ref[pl.ds(r0, LANES), :], n1, 0)

    @pl.when(t < s1)
    def _():
        _stage1(zy_ref, a_ref, g1_ref, t, it1, n1)

    @pl.when(t >= s1)
    def _():
        f2 = f2_ref[...]
        scale = 1.0 / (n1 * LANES)

        def body(i, c):
            k1 = (t - s1) * itm + i
            y = jnp.dot(f2, _gather_k1(a_ref, k1, n1), preferred_element_type=F32)
            o_ref[0, i] = (y * scale).astype(BF16)
            return c

        lax.fori_loop(0, itm, body, 0, unroll=DFT_UNROLL)


def _filter_spectra(kfull, g1_flt, f2, n1):
    n, cols = kfull.shape
    nslab = cols // LANES
    it1 = DFT_STEP_ITERS
    itm = min(DFT_STEP_ITERS, n1)
    s1 = LANES // it1
    sm = n1 // itm
    kern = functools.partial(_fdft_kernel, n1=n1, it1=it1, itm=itm, s1=s1)
    return pl.pallas_call(
        kern,
        grid=(nslab, s1 + sm),
        in_specs=[pl.BlockSpec((n, LANES), lambda s, t: (0, s)),
                  pl.BlockSpec((it1, 2 * n1, n1), lambda s, t: (jnp.minimum(t, s1 - 1), 0, 0)),
                  pl.BlockSpec((2 * LANES, 2 * LANES), lambda s, t: (0, 0))],
        out_specs=pl.BlockSpec((1, itm, 2 * LANES, LANES), lambda s, t: (s, jnp.maximum(t - s1, 0), 0, 0)),
        out_shape=jax.ShapeDtypeStruct((nslab, n1, 2 * LANES, LANES), BF16),
        scratch_shapes=[pltpu.VMEM((n1 * Z_PITCH, LANES), F32),
                        pltpu.VMEM((LANES * _a_pitch(n1), LANES), F32)],
        compiler_params=_cparams(("parallel", "arbitrary")),
        name="fdft",
    )(kfull, g1_flt, f2)


def _hyena_kernel(z_ref, gate_ref, skip_ref, g1_ref, kf_ref, g3_ref, f2_ref, f2i_ref, o_ref,
                  zy_ref, a_ref, *, n1, it1, itm, s1, sm):
    t = pl.program_id(1)
    h1 = n1 // 2
    pa = _a_pitch(n1)

    @pl.when(t == 0)
    def _():
        for bb in range(2):
            _load_blocks(zy_ref, lambda r0, bb=bb: z_ref[bb, 0, pl.ds(r0, LANES), :], h1, bb * h1)

    @pl.when(t < s1)
    def _():
        _stage1(zy_ref, a_ref, g1_ref, t, it1, n1)

    @pl.when(jnp.logical_and(t >= s1, t < s1 + sm))
    def _():
        f2 = f2_ref[...]
        f2i = f2i_ref[...]

        def body(i, c):
            k1 = (t - s1) * itm + i
            y = jnp.dot(f2, _gather_k1(a_ref, k1, n1), preferred_element_type=F32)
            kf = kf_ref[0, i].astype(F32)
            yr, yi = y[:LANES], y[LANES:]
            kr, ki = kf[:LANES], kf[LANES:]
            prod = jnp.concatenate([yr * kr - yi * ki, yr * ki + yi * kr], axis=0).astype(BF16)
            r = jnp.dot(f2i, prod, preferred_element_type=F32)
            a_ref[pl.ds(k1, LANES, stride=pa), :] = r[:LANES]
            a_ref[pl.ds(n1 + k1, LANES, stride=pa), :] = r[LANES:]
            return c

        lax.fori_loop(0, itm, body, 0, unroll=DFT_UNROLL)

    @pl.when(t >= s1 + sm)
    def _():
        def body(i, c):
            n2 = (t - s1 - sm) * it1 + i
            rhs = a_ref[pl.ds(pl.multiple_of(n2 * pa, SUBLANES), 2 * n1), :].astype(BF16)
            zy_ref[pl.ds(n2, n1, stride=Z_PITCH), :] = jnp.dot(g3_ref[i], rhs, preferred_element_type=F32)
            return c

        lax.fori_loop(0, it1, body, 0, unroll=DFT_UNROLL)

    @pl.when(t == s1 + sm + s1 - 1)
    def _():
        skip = skip_ref[...]
        for bb in range(2):
            def body(i, c, bb=bb):
                r0 = pl.multiple_of(i * LANES, LANES)
                d0 = pl.multiple_of((bb * h1 + i) * Z_PITCH, SUBLANES)
                y = zy_ref[pl.ds(d0, LANES), :]
                zz = z_ref[bb, 0, pl.ds(r0, LANES), :].astype(F32)
                gg = gate_ref[bb, 0, pl.ds(r0, LANES), :].astype(F32)
                o_ref[bb, 0, pl.ds(r0, LANES), :] = (gg * (y + skip * zz)).astype(BF16)
                return c

            lax.fori_loop(0, h1, body, 0)


def _hyena_order(z, z_off, gate, gate_off, skip, kf, kf_off, tables, n1):
    g1_sig, _, g3, f2, f2i = tables
    b, _, s, _ = z.shape
    width = skip.shape[1]
    nslab = width // LANES
    it1 = DFT_STEP_ITERS
    itm = min(DFT_STEP_ITERS, n1)
    s1 = LANES // it1
    sm = n1 // itm
    steps = 2 * s1 + sm
    kern = functools.partial(_hyena_kernel, n1=n1, it1=it1, itm=itm, s1=s1, sm=sm)
    one = pl.Buffered(1)
    return pl.pallas_call(
        kern,
        grid=((b // 2) * nslab, steps),
        in_specs=[pl.BlockSpec((2, 1, s, LANES), lambda g, t: (g // nslab, z_off + g % nslab, 0, 0),
                               pipeline_mode=one),
                  pl.BlockSpec((2, 1, s, LANES), lambda g, t: (g // nslab, gate_off + g % nslab, 0, 0),
                               pipeline_mode=one),
                  pl.BlockSpec((1, LANES), lambda g, t: (0, g % nslab)),
                  pl.BlockSpec((it1, 2 * n1, n1), lambda g, t: (jnp.minimum(t, s1 - 1), 0, 0)),
                  pl.BlockSpec((1, itm, 2 * LANES, LANES),
                               lambda g, t: (kf_off + g % nslab, jnp.clip(t - s1, 0, sm - 1), 0, 0)),
                  pl.BlockSpec((it1, n1, 2 * n1), lambda g, t: (jnp.clip(t - s1 - sm, 0, s1 - 1), 0, 0)),
                  pl.BlockSpec((2 * LANES, 2 * LANES), lambda g, t: (0, 0)),
                  pl.BlockSpec((2 * LANES, 2 * LANES), lambda g, t: (0, 0))],
        out_specs=pl.BlockSpec((2, 1, s, LANES), lambda g, t: (g // nslab, g % nslab, 0, 0)),
        out_shape=jax.ShapeDtypeStruct((b, nslab, s, LANES), BF16),
        scratch_shapes=[pltpu.VMEM((n1 * Z_PITCH, LANES), F32),
                        pltpu.VMEM((LANES * _a_pitch(n1), LANES), F32)],
        compiler_params=_cparams(("parallel", "arbitrary")),
        name="hyena",
    )(z, gate, skip, g1_sig, kf, g3, f2, f2i)


def _ffn_kernel(x_ref, xp_ref, xn_ref, a_ref, ap_ref, an_ref, z_ref, zp_ref, zn_ref, mod_ref, hn_ref, wo_ref,
                g_ref, wu_ref, cw_ref, cb_ref, wd_ref, fg_ref, o_ref, *, dff, chunks):
    ts = x_ref.shape[1]
    m = mod_ref[0]
    aw = ATTN_WIDTH
    za = jnp.concatenate([jnp.concatenate([zp_ref[0, sl], z_ref[0, sl], zn_ref[0, sl]], axis=0)
                          for sl in range(z_ref.shape[1])], axis=1)
    zn = _rms(za.astype(F32), hn_ref[...]).astype(BF16)
    mix = (jnp.dot(_with_halo(a_ref, ap_ref, an_ref), wo_ref[:aw, :], preferred_element_type=F32)
           + jnp.dot(zn, wo_ref[aw:, :], preferred_element_type=F32))
    x1 = _with_halo(x_ref, xp_ref, xn_ref) + m[2:3] * mix
    h = (_rms(x1, g_ref[...]) * (1.0 + m[4:5]) + m[3:4]).astype(BF16)
    hm = h[FFN_HALO:FFN_HALO + ts]
    masks = _edge_masks(ts)
    acc = jnp.zeros((ts, x1.shape[1]), F32)
    for c0, cf in chunks:
        a = jnp.dot(h, wu_ref[:, c0:c0 + cf], preferred_element_type=F32)
        gate = jnp.dot(hm, wu_ref[:, dff + c0:dff + c0 + cf], preferred_element_type=F32)
        ac = _conv3(a, cw_ref[:, c0:c0 + cf], cb_ref[:, c0:c0 + cf], ts, FFN_HALO, *masks)
        gl = 0.5 * ac * (1.0 + lax.erf(ac * (2.0 ** -0.5)))
        acc = acc + jnp.dot((gl * gate).astype(BF16), wd_ref[c0:c0 + cf, :], preferred_element_type=F32)
    y = x1[FFN_HALO:FFN_HALO + ts] + m[5:6] * acc
    o_ref[0] = _rms(y, fg_ref[...])


def _mxu_chunks(n, parts):
    tiles = n // MXU_DIM
    if tiles < parts or n % MXU_DIM:
        return ((0, n),)
    sizes = [(tiles // parts + (1 if i < tiles % parts else 0)) * MXU_DIM for i in range(parts)]
    return tuple((sum(sizes[:i]), sizes[i]) for i in range(parts))


def _ffn(x, attn, hz, modr, hy_norm, w_out_bf, norm_g, wu_bf, conv_w, conv_b, wd_bf, final_g):
    b, s, d = x.shape
    dff = wd_bf.shape[0]
    ts = min(512, s)
    kern = functools.partial(_ffn_kernel, dff=dff, chunks=_mxu_chunks(dff, 2))
    one = pl.Buffered(1)
    const = lambda shape, **kw: pl.BlockSpec(shape, lambda i, j: (0,) * len(shape), **kw)
    return pl.pallas_call(
        kern,
        grid=(b, s // ts),
        in_specs=(_halo_specs(ts, s, d, FFN_HALO) + _halo_specs(ts, s, attn.shape[2], FFN_HALO)
                  + _slab_halo_specs(ts, s, hz.shape[1], FFN_HALO) + [
                      pl.BlockSpec((1, 8, d), lambda i, j: (i, 0, 0)),
                      const(hy_norm.shape),
                      const(w_out_bf.shape, pipeline_mode=one),
                      const((1, d)),
                      const(wu_bf.shape, pipeline_mode=one),
                      const(conv_w.shape),
                      const((1, dff)),
                      const(wd_bf.shape, pipeline_mode=one),
                      const((1, d))]),
        out_specs=pl.BlockSpec((1, ts, d), lambda i, j: (i, j, 0)),
        out_shape=jax.ShapeDtypeStruct((b, s, d), F32),
        compiler_params=_cparams(("parallel", "parallel")),
        name="ffn",
    )(x, x, x, attn, attn, attn, hz, hz, hz, modr, hy_norm, w_out_bf, norm_g, wu_bf, conv_w,
      conv_b.reshape(1, dff), wd_bf, final_g)


def _rope_tables(s):
    nf = DIFF_DH // 4
    inv = ROPE_BASE ** (-jnp.arange(nf, dtype=F32) / nf)
    t = jnp.arange(s, dtype=jnp.int32)
    row = (t // GRID_W).astype(F32)[:, None] * inv
    col = (t % GRID_W).astype(F32)[:, None] * inv
    cos = jnp.concatenate([jnp.cos(row), jnp.cos(row), jnp.cos(col), jnp.cos(col)], axis=1)
    sin = jnp.concatenate([-jnp.sin(row), jnp.sin(row), -jnp.sin(col), jnp.sin(col)], axis=1)
    return jnp.tile(cos, (1, LANES // DIFF_DH)), jnp.tile(sin, (1, LANES // DIFF_DH))


def _filter_features(l):
    pos = jnp.concatenate([jnp.arange(l, dtype=jnp.int32), l - jnp.arange(l, dtype=jnp.int32)])
    valid = jnp.ones((2 * l,), F32).at[l].set(0.0)
    pos = jnp.where(pos == l, 0, pos).astype(F32)[None, :]
    tt = pos / (l - 1)
    w = 2.0 * math.pi * pos / l
    f = jnp.linspace(1e-4, HY_BANDS - 1, HY_BANDS, dtype=F32)[:, None]
    feat_t = jnp.concatenate([tt, jnp.cos(f * w), -jnp.sin(f * w)], axis=0)
    feat_t = jnp.pad(feat_t, ((0, LANES - feat_t.shape[0]), (0, 0)))
    tv = jnp.pad(jnp.stack([tt[0], valid], axis=1), ((0, 0), (0, 6)))
    return feat_t, tv


def kernel(x, c, ctx, c_ctx, w_mod, b_mod, norm_mix, norm_ffn, w_in, lam_q1, lam_k1, lam_q2, lam_k2, subln, hy_conv_w, hy_conv_b, hy_w1, hy_b1, hy_w2, hy_b2, hy_w3, hy_b3, hy_w4, hy_freq, hy_skip, hy_norm, w_out, ffn_w_up, ffn_conv_w, ffn_conv_b, ffn_w_down, final_norm):
    b, s, d = x.shape
    assert w_mod.shape[0] == 1 and b % 2 == 0 and (2 * s) % (LANES * 16) == 0
    assert s % ctx.shape[1] == 0 and ctx.shape[1] % LANES == 0
    aw = ATTN_WIDTH
    hw = hy_skip.shape[2]
    n1 = 2 * s // LANES

    rows = ((b + 1 + 7) // 8) * 8
    cvec = jnp.zeros((rows, d), F32).at[:b].set(c).at[b].set(c_ctx)
    mod = _modulation(cvec, w_mod[0], b_mod[0])[:b + 1]
    modr = jnp.pad(mod.reshape(b + 1, N_MOD, d), ((0, 0), (0, 8 - N_MOD), (0, 0)))

    w_in_bf = w_in[0].astype(BF16)
    cos_t, sin_t = _rope_tables(s)
    g_mix = norm_mix[0].reshape(1, d)
    w_vt_bf = w_in_bf[:, 2 * aw:3 * aw].T
    q, k_all, vt_all, u = _inproj(x, modr, g_mix, w_in_bf, w_vt_bf, cos_t, sin_t, hy_conv_w[0], hy_conv_b[0],
                                  ctx.shape[1])
    k_all, vt_all = _ctxproj(ctx, modr, g_mix, w_in_bf[:, aw:2 * aw], w_vt_bf, k_all, vt_all)
    lamp = jnp.concatenate([lam_q1, lam_k1, lam_q2, lam_k2], axis=0)
    attn = _attention(q, k_all, vt_all, lamp, subln[0].reshape(1, DIFF_VD))

    p = {"hy_w1": hy_w1[0], "hy_b1": hy_b1[0], "hy_w2": hy_w2[0], "hy_b2": hy_b2[0],
         "hy_w3": hy_w3[0], "hy_b3": hy_b3[0], "hy_freq": hy_freq[0]}
    fo = hy_w4.shape[1]
    w4r = hy_w4[0].reshape(fo, HY_ORDER, 2, hw).transpose(2, 0, 1, 3).reshape(2, fo, HY_ORDER * hw)
    min_decay = math.log(HY_DECAY_TARGET) / HY_FAST_DECAY_PCT
    max_decay = math.log(HY_DECAY_TARGET) / HY_SLOW_DECAY_PCT
    deltas = jnp.abs(jnp.linspace(min_decay, max_decay, HY_ORDER * 2 * hw, dtype=F32))
    deltas = deltas.reshape(HY_ORDER, 2, hw).transpose(1, 0, 2).reshape(2, 1, HY_ORDER * hw)
    kfull = _hyena_filters(*_filter_features(s), p, w4r, deltas)
    tables = _dft_tables(n1)
    kf = _filter_spectra(kfull, tables[1], tables[3], n1)

    nslab = hw // LANES
    z = _hyena_order(u, 0, u, nslab, hy_skip[0, 0:1], kf, 0, tables, n1)
    z = _hyena_order(z, 0, u, 2 * nslab, hy_skip[0, 1:2], kf, nslab, tables, n1)

    return _ffn(x, attn, z, modr, hy_norm[0].reshape(1, hw), w_out[0].astype(BF16), norm_ffn[0].reshape(1, d),
                ffn_w_up[0].astype(BF16), ffn_conv_w[0], ffn_conv_b[0], ffn_w_down[0].astype(BF16),
                final_norm.reshape(1, d))
```

```python
import functools
import math

import jax
import jax.numpy as jnp
from jax import lax
from jax.experimental import pallas as pl
from jax.experimental.pallas import tpu as pltpu

F32 = jnp.float32
BF16 = jnp.bfloat16
HIGHEST = lax.Precision.HIGHEST

EPS = 1e-6
N_MOD = 6
GRID_W = 64
DIFF_HEADS = 4
DIFF_DH = 64
DIFF_VD = 128
ATTN_WIDTH = DIFF_HEADS * DIFF_VD
HY_ORDER = 2
HY_BANDS = 16
HY_DECAY_TARGET = 1e-2
HY_FAST_DECAY_PCT = 0.3
HY_SLOW_DECAY_PCT = 1.5
ROPE_BASE = 10000.0
LAM_INIT = 0.8 - 0.6 * math.exp(-0.3 * 0)

LANES = 128
SUBLANES = 8
MXU_DIM = 256
FFN_HALO = 16
Z_PITCH = LANES + SUBLANES
DFT_STEP_ITERS = 32
DFT_UNROLL = True
VMEM_LIMIT = 56 * 1024 * 1024
NEG_BIG = -1e30
NT_DIMS = (((1,), (1,)), ((), ()))
ONES_ROWS = 16
Q_SCALE = DIFF_DH ** -0.5 * math.log2(math.e)


def _cparams(sem):
    return pltpu.CompilerParams(dimension_semantics=sem, vmem_limit_bytes=VMEM_LIMIT)


def _rms(x, g):
    return x * lax.rsqrt(jnp.mean(x * x, axis=-1, keepdims=True) + EPS) * g


def _mod_kernel(c_ref, w_ref, b_ref, o_ref):
    c = c_ref[...]
    s = c * jax.nn.sigmoid(c)
    o_ref[...] = jnp.dot(s, w_ref[...], preferred_element_type=F32, precision=HIGHEST) + b_ref[...]


def _modulation(cvec, w_mod, b_mod):
    rows, d = cvec.shape
    cols = w_mod.shape[1]
    bc = 1024
    return pl.pallas_call(
        _mod_kernel,
        grid=(cols // bc,),
        in_specs=[pl.BlockSpec((rows, d), lambda j: (0, 0)),
                  pl.BlockSpec((d, bc), lambda j: (0, j)),
                  pl.BlockSpec((1, bc), lambda j: (0, j))],
        out_specs=pl.BlockSpec((rows, bc), lambda j: (0, j)),
        out_shape=jax.ShapeDtypeStruct((rows, cols), F32),
        compiler_params=_cparams(("arbitrary",)),
        name="mod",
    )(cvec, w_mod, b_mod.reshape(1, cols))


def _rope(t, cos, sin_signed, first_half):
    sw = jnp.where(first_half, pltpu.roll(t, LANES - 16, 1), pltpu.roll(t, 16, 1))
    return t * cos + sw * sin_signed


def _halo_specs(ts, s, d, halo):
    nh = ts // halo
    nblk = s // halo
    return [pl.BlockSpec((1, ts, d), lambda i, j: (i, j, 0)),
            pl.BlockSpec((1, halo, d), lambda i, j: (i, jnp.maximum(j * nh - 1, 0), 0)),
            pl.BlockSpec((1, halo, d), lambda i, j: (i, jnp.minimum((j + 1) * nh, nblk - 1), 0))]


def _slab_halo_specs(ts, s, nslab, halo):
    nh = ts // halo
    nblk = s // halo
    return [pl.BlockSpec((1, nslab, ts, LANES), lambda i, j: (i, 0, j, 0)),
            pl.BlockSpec((1, nslab, halo, LANES), lambda i, j: (i, 0, jnp.maximum(j * nh - 1, 0), 0)),
            pl.BlockSpec((1, nslab, halo, LANES), lambda i, j: (i, 0, jnp.minimum((j + 1) * nh, nblk - 1), 0))]


def _with_halo(ref, prev_ref, next_ref):
    return jnp.concatenate([prev_ref[0], ref[0], next_ref[0]], axis=0)


def _conv3(a, w, bias, ts, halo, no_prev, no_next):
    rows = ts + 2 * halo
    prev = jnp.where(no_prev, 0.0, pltpu.roll(a, 1, 0)[halo:halo + ts])
    nxt = jnp.where(no_next, 0.0, pltpu.roll(a, rows - 1, 0)[halo:halo + ts])
    return prev * w[0:1] + a[halo:halo + ts] * w[1:2] + nxt * w[2:3] + bias


def _edge_masks(ts):
    j = pl.program_id(1)
    row = lax.broadcasted_iota(jnp.int32, (ts, 1), 0)
    no_prev = jnp.logical_and(j == 0, row == 0)
    no_next = jnp.logical_and(j == pl.num_programs(1) - 1, row == ts - 1)
    return no_prev, no_next


def _inproj_kernel(x_ref, xp_ref, xn_ref, mod_ref, g_ref, w_ref, wvt_ref, cos_ref, sin_ref, cw_ref, cb_ref,
                   q_ref, k_ref, vt_ref, u_ref):
    ts = x_ref.shape[1]
    m = mod_ref[0]
    xa = _with_halo(x_ref, xp_ref, xn_ref)
    ha = (_rms(xa, g_ref[...]) * (1.0 + m[1:2]) + m[0:1]).astype(BF16)
    h = ha[SUBLANES:SUBLANES + ts]
    cos = cos_ref[...]
    sin = sin_ref[...]
    lane = lax.broadcasted_iota(jnp.int32, cos.shape, 1)
    first_half = (lane % 32) < 16
    aw = ATTN_WIDTH
    q = jnp.dot(h, w_ref[:, :aw], preferred_element_type=F32)
    k = jnp.dot(h, w_ref[:, aw:2 * aw], preferred_element_type=F32)
    for s in range(aw // LANES):
        c0 = s * LANES
        q_ref[0, :, c0:c0 + LANES] = (_rope(q[:, c0:c0 + LANES], cos, sin, first_half) * Q_SCALE).astype(BF16)
        k_ref[0, :, c0:c0 + LANES] = _rope(k[:, c0:c0 + LANES], cos, sin, first_half).astype(BF16)
    vt_ref[0] = lax.dot_general(wvt_ref[...], h, NT_DIMS, preferred_element_type=F32).astype(BF16)
    hy = jnp.dot(ha, w_ref[:, 3 * aw:], preferred_element_type=F32)
    u = _conv3(hy, cw_ref[...], cb_ref[...], ts, SUBLANES, *_edge_masks(ts)).astype(BF16)
    for sl in range(u_ref.shape[1]):
        u_ref[0, sl] = u[:, sl * LANES:(sl + 1) * LANES]


def _inproj(x, modr, norm_g, w_in_bf, w_vt_bf, cos_t, sin_t, conv_w, conv_b, ctx_len):
    b, s, d = x.shape
    ts = min(512, s)
    aw = ATTN_WIDTH
    hyc = w_in_bf.shape[1] - 3 * aw
    sk = s + ctx_len
    return pl.pallas_call(
        _inproj_kernel,
        grid=(b, s // ts),
        in_specs=_halo_specs(ts, s, d, SUBLANES) + [
                  pl.BlockSpec((1, 8, d), lambda i, j: (i, 0, 0)),
                  pl.BlockSpec((1, d), lambda i, j: (0, 0)),
                  pl.BlockSpec(w_in_bf.shape, lambda i, j: (0, 0)),
                  pl.BlockSpec(w_vt_bf.shape, lambda i, j: (0, 0)),
                  pl.BlockSpec((ts, LANES), lambda i, j: (j, 0)),
                  pl.BlockSpec((ts, LANES), lambda i, j: (j, 0)),
                  pl.BlockSpec(conv_w.shape, lambda i, j: (0, 0)),
                  pl.BlockSpec((1, hyc), lambda i, j: (0, 0))],
        out_specs=[pl.BlockSpec((1, ts, aw), lambda i, j: (i, j, 0)),
                   pl.BlockSpec((1, ts, aw), lambda i, j: (i, j, 0)),
                   pl.BlockSpec((1, aw, ts), lambda i, j: (i, 0, j)),
                   pl.BlockSpec((1, hyc // LANES, ts, LANES), lambda i, j: (i, 0, j, 0))],
        out_shape=[jax.ShapeDtypeStruct((b, s, aw), BF16), jax.ShapeDtypeStruct((b, sk, aw), BF16),
                   jax.ShapeDtypeStruct((b, aw, sk), BF16),
                   jax.ShapeDtypeStruct((b, hyc // LANES, s, LANES), BF16)],
        compiler_params=_cparams(("parallel", "parallel")),
        name="inproj",
    )(x, x, x, modr, norm_g, w_in_bf, w_vt_bf, cos_t, sin_t, conv_w, conv_b.reshape(1, hyc))


def _ctxproj_kernel(x_ref, mod_ref, g_ref, wk_ref, wvt_ref, k_in_ref, vt_in_ref, k_ref, vt_ref):
    del k_in_ref, vt_in_ref
    m = mod_ref[0]
    h = (_rms(x_ref[0], g_ref[...]) * (1.0 + m[1:2]) + m[0:1]).astype(BF16)
    k_ref[0] = jnp.dot(h, wk_ref[...], preferred_element_type=F32).astype(BF16)
    vt_ref[0] = lax.dot_general(wvt_ref[...], h, NT_DIMS, preferred_element_type=F32).astype(BF16)


def _ctxproj(ctx, modr, norm_g, w_k_bf, w_vt_bf, k_all, vt_all):
    b, c, d = ctx.shape
    aw = ATTN_WIDTH
    nb = modr.shape[0] - 1
    blk = (k_all.shape[1] - c) // c
    return pl.pallas_call(
        _ctxproj_kernel,
        grid=(b,),
        in_specs=[pl.BlockSpec((1, c, d), lambda i: (i, 0, 0)),
                  pl.BlockSpec((1, 8, d), lambda i: (nb, 0, 0)),
                  pl.BlockSpec((1, d), lambda i: (0, 0)),
                  pl.BlockSpec(w_k_bf.shape, lambda i: (0, 0)),
                  pl.BlockSpec(w_vt_bf.shape, lambda i: (0, 0)),
                  pl.BlockSpec(memory_space=pl.ANY),
                  pl.BlockSpec(memory_space=pl.ANY)],
        out_specs=[pl.BlockSpec((1, c, aw), lambda i: (i, blk, 0)), pl.BlockSpec((1, aw, c), lambda i: (i, 0, blk))],
        out_shape=[jax.ShapeDtypeStruct(k_all.shape, BF16), jax.ShapeDtypeStruct(vt_all.shape, BF16)],
        input_output_aliases={5: 0, 6: 1},
        compiler_params=_cparams(("parallel",)),
        name="ctxproj",
    )(ctx, modr, norm_g, w_k_bf, w_vt_bf, k_all, vt_all)


def _attn_kernel(q_ref, k_ref, vt_ref, lam_ref, sub_ref, o_ref, qm_scr, sa_scr, sb_scr, acc_scr,
                 *, tq, tk, nq, nkv):
    q = q_ref[0]
    lane = lax.broadcasted_iota(jnp.int32, q.shape, 1)
    zero = jnp.zeros_like(q)
    qm_scr[0] = jnp.where(lane < DIFF_DH, q, zero)
    qm_scr[1] = jnp.where(lane >= DIFF_DH, q, zero)
    acc_scr[...] = jnp.zeros(acc_scr.shape, F32)
    ones = jnp.ones((ONES_ROWS, tk), BF16)
    steps = nq * nkv
    neg = jnp.full((1, tq), NEG_BIG, F32)
    lp = lam_ref[...]
    lam = (jnp.exp(jnp.sum(lp[0:1] * lp[1:2], axis=-1, keepdims=True))
           - jnp.exp(jnp.sum(lp[2:3] * lp[3:4], axis=-1, keepdims=True)) + LAM_INIT)

    def restart(t, m):
        return [jnp.where(t % nkv == 0, neg, mm) for mm in m]

    def scores(t, s_scr, m_cur):
        t = jnp.minimum(t, steps - 1)
        q0 = pl.multiple_of((t // nkv) * tq, tq)
        kj = k_ref[0, pl.ds(pl.multiple_of((t % nkv) * tk, tk), tk), :]
        m_base = restart(t, m_cur)
        m_next = []
        for mp in range(2):
            s = lax.dot_general(kj, qm_scr[mp, pl.ds(q0, tq), :], NT_DIMS, preferred_element_type=F32)
            s_scr[mp] = s
            m_next.append(jnp.maximum(m_base[mp], jnp.max(s, axis=0, keepdims=True)))
        return m_next

    def accumulate(t, s_scr, m_prev, m_cur):
        j = t % nkv
        vt = vt_ref[0, :, pl.ds(pl.multiple_of(j * tk, tk), tk)]
        vext = jnp.concatenate([vt, ones], axis=0)
        m_old = restart(t, m_prev)
        for mp in range(2):
            p = jnp.exp2((s_scr[mp] - m_cur[mp]).astype(BF16))
            alpha = jnp.exp2(m_old[mp] - m_cur[mp])
            acc_scr[mp] = alpha * acc_scr[mp] + jnp.dot(vext, p, preferred_element_type=F32)

        @pl.when(j == nkv - 1)
        def _():
            a0 = acc_scr[0]
            a1 = acc_scr[1]
            vd = DIFF_VD
            o = a0[:vd] / a0[vd:vd + 1] - lam * (a1[:vd] / a1[vd:vd + 1])
            on = o * lax.rsqrt(jnp.mean(o * o, axis=0, keepdims=True) + EPS)
            q0 = pl.multiple_of((t // nkv) * tq, tq)
            o_ref[0, pl.ds(q0, tq), :] = (on.T * sub_ref[...] * (1.0 - LAM_INIT)).astype(BF16)

    m_first = scores(0, sa_scr, [neg, neg])

    def pair(i, carry):
        m_prev, m_cur = list(carry[:2]), list(carry[2:])
        t = 2 * i
        m_nxt = scores(t + 1, sb_scr, m_cur)
        accumulate(t, sa_scr, m_prev, m_cur)
        m_nn = scores(t + 2, sa_scr, m_nxt)
        accumulate(t + 1, sb_scr, m_cur, m_nxt)
        return (*m_nxt, *m_nn)

    carry = lax.fori_loop(0, steps // 2, pair, (neg, neg, *m_first))
    if steps % 2:
        accumulate(steps - 1, sa_scr, list(carry[:2]), list(carry[2:]))


def _pick_tile(n, cands):
    for c in cands:
        if n % c == 0:
            return c
    raise ValueError(f"no tile for {n}")


def _attention(q, k_all, vt_all, lamp, subln):
    b, s, aw = q.shape
    sk = k_all.shape[1]
    tq = min(1024, s)
    tk = _pick_tile(sk, (1408, 768, 512, 384, 256, 128))
    kern = functools.partial(_attn_kernel, tq=tq, tk=tk, nq=s // tq, nkv=sk // tk)
    return pl.pallas_call(
        kern,
        grid=(b, DIFF_HEADS),
        in_specs=[pl.BlockSpec((1, s, LANES), lambda i, h: (i, 0, h)),
                  pl.BlockSpec((1, sk, LANES), lambda i, h: (i, 0, h)),
                  pl.BlockSpec((1, DIFF_VD, sk), lambda i, h: (i, h, 0)),
                  pl.BlockSpec(lamp.shape, lambda i, h: (0, 0)),
                  pl.BlockSpec((1, DIFF_VD), lambda i, h: (0, 0))],
        out_specs=pl.BlockSpec((1, s, LANES), lambda i, h: (i, 0, h)),
        out_shape=jax.ShapeDtypeStruct((b, s, aw), BF16),
        scratch_shapes=[pltpu.VMEM((2, s, LANES), BF16),
                        pltpu.VMEM((2, tk, tq), F32),
                        pltpu.VMEM((2, tk, tq), F32),
                        pltpu.VMEM((2, DIFF_VD + ONES_ROWS, tq), F32)],
        compiler_params=_cparams(("parallel", "parallel")),
        name="attn",
    )(q, k_all, vt_all, lamp, subln)


def _filt_kernel(ft_ref, tv_ref, w1t_ref, b1_ref, w2t_ref, b2_ref, w3t_ref, b3_ref, fr_ref, w4_ref, dl_ref, o_ref):
    dot = functools.partial(jnp.dot, preferred_element_type=F32, precision=HIGHEST)
    fr = fr_ref[...]
    h = jnp.sin(fr * (dot(w1t_ref[...], ft_ref[...]) + b1_ref[...]))
    h = jnp.sin(fr * (dot(w2t_ref[...], h) + b2_ref[...]))
    h = jnp.sin(fr * (dot(w3t_ref[...], h) + b3_ref[...]))
    k = lax.dot_general(h, w4_ref[0], (((0,), (0,)), ((), ())), preferred_element_type=F32,
                        precision=HIGHEST)
    tv = tv_ref[...]
    o_ref[...] = k * jnp.exp(-tv[:, 0:1] * dl_ref[0]) * tv[:, 1:2]


def _hyena_filters(feat_t, tv, p, w4r, deltas):
    n = feat_t.shape[1]
    tr = min(1024, n // 2)
    half_steps = (n // 2) // tr
    fo = p["hy_w2"].shape[0]
    cols = w4r.shape[2]
    const = lambda shape: pl.BlockSpec(shape, lambda i: (0,) * len(shape))
    w1t = jnp.zeros((fo, LANES), F32).at[:, :p["hy_w1"].shape[0]].set(p["hy_w1"].T)
    col = lambda v: v.reshape(fo, 1)
    return pl.pallas_call(
        _filt_kernel,
        grid=(n // tr,),
        in_specs=[pl.BlockSpec((LANES, tr), lambda i: (0, i)),
                  pl.BlockSpec((tr, tv.shape[1]), lambda i: (i, 0)),
                  const((fo, LANES)), const((fo, 1)), const((fo, fo)), const((fo, 1)),
                  const((fo, fo)), const((fo, 1)), const((fo, 1)),
                  pl.BlockSpec((1, fo, cols), lambda i: (i // half_steps, 0, 0)),
                  pl.BlockSpec((1, 1, cols), lambda i: (i // half_steps, 0, 0))],
        out_specs=pl.BlockSpec((tr, cols), lambda i: (i, 0)),
        out_shape=jax.ShapeDtypeStruct((n, cols), F32),
        compiler_params=_cparams(("parallel",)),
        name="filt",
    )(feat_t, tv, w1t, col(p["hy_b1"]), p["hy_w2"].T, col(p["hy_b2"]), p["hy_w3"].T, col(p["hy_b3"]),
      col(p["hy_freq"]), w4r, deltas)


def _dft_tables(n1):
    n2 = LANES
    n = n1 * n2
    h1 = n1 // 2
    i2 = jnp.arange(n2, dtype=jnp.int32)
    i1 = jnp.arange(n1, dtype=jnp.int32)
    at = ((i2[:, None] * i1[None, :]) % n).astype(F32) * (-2.0 * math.pi / n)
    a1 = ((i1[:, None] * i1[None, :]) % n1).astype(F32) * (-2.0 * math.pi / n1)
    tr, ti = jnp.cos(at)[:, :, None], jnp.sin(at)[:, :, None]
    f1r, f1i = jnp.cos(a1)[None], jnp.sin(a1)[None]
    gr, gi = tr * f1r - ti * f1i, tr * f1i + ti * f1r
    grh, gih = gr[:, :, :h1], gi[:, :, :h1]
    g1_sig = jnp.concatenate([jnp.concatenate([grh, -gih], 2), jnp.concatenate([gih, grh], 2)], 1)
    g1_flt = jnp.concatenate([gr, gi], 1)
    qr, qi = jnp.swapaxes(grh, 1, 2), -jnp.swapaxes(gih, 1, 2)
    g3 = jnp.concatenate([jnp.concatenate([qr, -qi], 2), jnp.concatenate([qi, qr], 2)], 1)
    a2 = ((i2[:, None] * i2[None, :]) % n2).astype(F32) * (-2.0 * math.pi / n2)
    fr, fi = jnp.cos(a2), jnp.sin(a2)
    f2 = jnp.concatenate([jnp.concatenate([fr, -fi], 1), jnp.concatenate([fi, fr], 1)], 0)
    f2i = jnp.concatenate([jnp.concatenate([fr, fi], 1), jnp.concatenate([-fi, fr], 1)], 0)
    return (g1_sig.astype(BF16), g1_flt.astype(BF16), g3.astype(BF16), f2.astype(BF16), f2i.astype(BF16))


def _a_pitch(n1):
    return 2 * n1 + SUBLANES


def _stage1(zy_ref, a_ref, g_ref, step, it, n1):
    pa = _a_pitch(n1)

    def body(i, c):
        n2 = step * it + i
        rhs = zy_ref[pl.ds(n2, n1, stride=Z_PITCH), :].astype(BF16)
        a_ref[pl.ds(pl.multiple_of(n2 * pa, SUBLANES), 2 * n1), :] = jnp.dot(
            g_ref[i], rhs, preferred_element_type=F32)
        return c

    lax.fori_loop(0, it, body, 0, unroll=DFT_UNROLL)


def _gather_k1(a_ref, k1, n1):
    pa = _a_pitch(n1)
    re = a_ref[pl.ds(k1, LANES, stride=pa), :]
    im = a_ref[pl.ds(n1 + k1, LANES, stride=pa), :]
    return jnp.concatenate([re, im], axis=0).astype(BF16)


def _load_blocks(dst_ref, src, nblk, base):
    def body(i, c):
        r0 = pl.multiple_of(i * LANES, LANES)
        d0 = pl.multiple_of((base + i) * Z_PITCH, SUBLANES)
        dst_ref[pl.ds(d0, LANES), :] = src(r0).astype(F32)
        return c

    lax.fori_loop(0, nblk, body, 0)


def _fdft_kernel(k_ref, g1_ref, f2_ref, o_ref, zy_ref, a_ref, *, n1, it1, itm, s1):
    t = pl.program_id(1)

    @pl.when(t == 0)
    def _():
        _load_blocks(zy_ref, lambda r0: k_ref[pl.ds(r0, LANES), :], n1, 0)

    @pl.when(t < s1)
    def _():
        _stage1(zy_ref, a_ref, g1_ref, t, it1, n1)

    @pl.when(t >= s1)
    def _():
        f2 = f2_ref[...]
        scale = 1.0 / (n1 * LANES)

        def body(i, c):
            k1 = (t - s1) * itm + i
            y = jnp.dot(f2, _gather_k1(a_ref, k1, n1), preferred_element_type=F32)
            o_ref[0, i] = (y * scale).astype(BF16)
            return c

        lax.fori_loop(0, itm, body, 0, unroll=DFT_UNROLL)


def _filter_spectra(kfull, g1_flt, f2, n1):
    n, cols = kfull.shape
    nslab = cols // LANES
    it1 = DFT_STEP_ITERS
    itm = min(DFT_STEP_ITERS, n1)
    s1 = LANES // it1
    sm = n1 // itm
    kern = functools.partial(_fdft_kernel, n1=n1, it1=it1, itm=itm, s1=s1)
    return pl.pallas_call(
        kern,
        grid=(nslab, s1 + sm),
        in_specs=[pl.BlockSpec((n, LANES), lambda s, t: (0, s)),
                  pl.BlockSpec((it1, 2 * n1, n1), lambda s, t: (jnp.minimum(t, s1 - 1), 0, 0)),
                  pl.BlockSpec((2 * LANES, 2 * LANES), lambda s, t: (0, 0))],
        out_specs=pl.BlockSpec((1, itm, 2 * LANES, LANES), lambda s, t: (s, jnp.maximum(t - s1, 0), 0, 0)),
        out_shape=jax.ShapeDtypeStruct((nslab, n1, 2 * LANES, LANES), BF16),
        scratch_shapes=[pltpu.VMEM((n1 * Z_PITCH, LANES), F32),
                        pltpu.VMEM((LANES * _a_pitch(n1), LANES), F32)],
        compiler_params=_cparams(("parallel", "arbitrary")),
        name="fdft",
    )(kfull, g1_flt, f2)


def _hyena_kernel(z_ref, gate_ref, skip_ref, g1_ref, kf_ref, g3_ref, f2_ref, f2i_ref, o_ref,
                  zy_ref, a_ref, *, n1, it1, itm, s1, sm):
    t = pl.program_id(1)
    h1 = n1 // 2
    pa = _a_pitch(n1)

    @pl.when(t == 0)
    def _():
        for bb in range(2):
            _load_blocks(zy_ref, lambda r0, bb=bb: z_ref[bb, 0, pl.ds(r0, LANES), :], h1, bb * h1)

    @pl.when(t < s1)
    def _():
        _stage1(zy_ref, a_ref, g1_ref, t, it1, n1)

    @pl.when(jnp.logical_and(t >= s1, t < s1 + sm))
    def _():
        f2 = f2_ref[...]
        f2i = f2i_ref[...]

        def body(i, c):
            k1 = (t - s1) * itm + i
            y = jnp.dot(f2, _gather_k1(a_ref, k1, n1), preferred_element_type=F32)
            kf = kf_ref[0, i].astype(F32)
            yr, yi = y[:LANES], y[LANES:]
            kr, ki = kf[:LANES], kf[LANES:]
            prod = jnp.concatenate([yr * kr - yi * ki, yr * ki + yi * kr], axis=0).astype(BF16)
            r = jnp.dot(f2i, prod, preferred_element_type=F32)
            a_ref[pl.ds(k1, LANES, stride=pa), :] = r[:LANES]
            a_ref[pl.ds(n1 + k1, LANES, stride=pa), :] = r[LANES:]
            return c

        lax.fori_loop(0, itm, body, 0, unroll=DFT_UNROLL)

    @pl.when(t >= s1 + sm)
    def _():
        def body(i, c):
            n2 = (t - s1 - sm) * it1 + i
            rhs = a_ref[pl.ds(pl.multiple_of(n2 * pa, SUBLANES), 2 * n1), :].astype(BF16)
            zy_ref[pl.ds(n2, n1, stride=Z_PITCH), :] = jnp.dot(g3_ref[i], rhs, preferred_element_type=F32)
            return c

        lax.fori_loop(0, it1, body, 0, unroll=DFT_UNROLL)

    @pl.when(t == s1 + sm + s1 - 1)
    def _():
        skip = skip_ref[...]
        for bb in range(2):
            def body(i, c, bb=bb):
                r0 = pl.multiple_of(i * LANES, LANES)
                d0 = pl.multiple_of((bb * h1 + i) * Z_PITCH, SUBLANES)
                y = zy_ref[pl.ds(d0, LANES), :]
                zz = z_ref[bb, 0, pl.ds(r0, LANES), :].astype(F32)
                gg = gate_ref[bb, 0, pl.ds(r0, LANES), :].astype(F32)
                o_ref[bb, 0, pl.ds(r0, LANES), :] = (gg * (y + skip * zz)).astype(BF16)
                return c

            lax.fori_loop(0, h1, body, 0)


def _hyena_order(z, z_off, gate, gate_off, skip, kf, kf_off, tables, n1):
    g1_sig, _, g3, f2, f2i = tables
    b, _, s, _ = z.shape
    width = skip.shape[1]
    nslab = width // LANES
    it1 = DFT_STEP_ITERS
    itm = min(DFT_STEP_ITERS, n1)
    s1 = LANES // it1
    sm = n1 // itm
    steps = 2 * s1 + sm
    kern = functools.partial(_hyena_kernel, n1=n1, it1=it1, itm=itm, s1=s1, sm=sm)
    one = pl.Buffered(1)
    return pl.pallas_call(
        kern,
        grid=((b // 2) * nslab, steps),
        in_specs=[pl.BlockSpec((2, 1, s, LANES), lambda g, t: (g // nslab, z_off + g % nslab, 0, 0),
                               pipeline_mode=one),
                  pl.BlockSpec((2, 1, s, LANES), lambda g, t: (g // nslab, gate_off + g % nslab, 0, 0),
                               pipeline_mode=one),
                  pl.BlockSpec((1, LANES), lambda g, t: (0, g % nslab)),
                  pl.BlockSpec((it1, 2 * n1, n1), lambda g, t: (jnp.minimum(t, s1 - 1), 0, 0)),
                  pl.BlockSpec((1, itm, 2 * LANES, LANES),
                               lambda g, t: (kf_off + g % nslab, jnp.clip(t - s1, 0, sm - 1), 0, 0)),
                  pl.BlockSpec((it1, n1, 2 * n1), lambda g, t: (jnp.clip(t - s1 - sm, 0, s1 - 1), 0, 0)),
                  pl.BlockSpec((2 * LANES, 2 * LANES), lambda g, t: (0, 0)),
                  pl.BlockSpec((2 * LANES, 2 * LANES), lambda g, t: (0, 0))],
        out_specs=pl.BlockSpec((2, 1, s, LANES), lambda g, t: (g // nslab, g % nslab, 0, 0)),
        out_shape=jax.ShapeDtypeStruct((b, nslab, s, LANES), BF16),
        scratch_shapes=[pltpu.VMEM((n1 * Z_PITCH, LANES), F32),
                        pltpu.VMEM((LANES * _a_pitch(n1), LANES), F32)],
        compiler_params=_cparams(("parallel", "arbitrary")),
        name="hyena",
    )(z, gate, skip, g1_sig, kf, g3, f2, f2i)


def _ffn_kernel(x_ref, xp_ref, xn_ref, a_ref, ap_ref, an_ref, z_ref, zp_ref, zn_ref, mod_ref, hn_ref, wo_ref,
                g_ref, wu_ref, cw_ref, cb_ref, wd_ref, fg_ref, o_ref, *, dff, chunks):
    ts = x_ref.shape[1]
    m = mod_ref[0]
    aw = ATTN_WIDTH
    za = jnp.concatenate([jnp.concatenate([zp_ref[0, sl], z_ref[0, sl], zn_ref[0, sl]], axis=0)
                          for sl in range(z_ref.shape[1])], axis=1)
    zn = _rms(za.astype(F32), hn_ref[...]).astype(BF16)
    mix = (jnp.dot(_with_halo(a_ref, ap_ref, an_ref), wo_ref[:aw, :], preferred_element_type=F32)
           + jnp.dot(zn, wo_ref[aw:, :], preferred_element_type=F32))
    x1 = _with_halo(x_ref, xp_ref, xn_ref) + m[2:3] * mix
    h = (_rms(x1, g_ref[...]) * (1.0 + m[4:5]) + m[3:4]).astype(BF16)
    hm = h[FFN_HALO:FFN_HALO + ts]
    masks = _edge_masks(ts)
    acc = jnp.zeros((ts, x1.shape[1]), F32)
    for c0, cf in chunks:
        a = jnp.dot(h, wu_ref[:, c0:c0 + cf], preferred_element_type=F32)
        gate = jnp.dot(hm, wu_ref[:, dff + c0:dff + c0 + cf], preferred_element_type=F32)
        ac = _conv3(a, cw_ref[:, c0:c0 + cf], cb_ref[:, c0:c0 + cf], ts, FFN_HALO, *masks)
        gl = 0.5 * ac * (1.0 + lax.erf(ac * (2.0 ** -0.5)))
        acc = acc + jnp.dot((gl * gate).astype(BF16), wd_ref[c0:c0 + cf, :], preferred_element_type=F32)
    y = x1[FFN_HALO:FFN_HALO + ts] + m[5:6] * acc
    o_ref[0] = _rms(y, fg_ref[...])


def _mxu_chunks(n, parts):
    tiles = n // MXU_DIM
    if tiles < parts or n % MXU_DIM:
        return ((0, n),)
    sizes = [(tiles // parts + (1 if i < tiles % parts else 0)) * MXU_DIM for i in range(parts)]
    return tuple((sum(sizes[:i]), sizes[i]) for i in range(parts))


def _ffn(x, attn, hz, modr, hy_norm, w_out_bf, norm_g, wu_bf, conv_w, conv_b, wd_bf, final_g):
    b, s, d = x.shape
    dff = wd_bf.shape[0]
    ts = min(512, s)
    kern = functools.partial(_ffn_kernel, dff=dff, chunks=_mxu_chunks(dff, 2))
    one = pl.Buffered(1)
    const = lambda shape, **kw: pl.BlockSpec(shape, lambda i, j: (0,) * len(shape), **kw)
    return pl.pallas_call(
        kern,
        grid=(b, s // ts),
        in_specs=(_halo_specs(ts, s, d, FFN_HALO) + _halo_specs(ts, s, attn.shape[2], FFN_HALO)
                  + _slab_halo_specs(ts, s, hz.shape[1], FFN_HALO) + [
                      pl.BlockSpec((1, 8, d), lambda i, j: (i, 0, 0)),
                      const(hy_norm.shape),
                      const(w_out_bf.shape, pipeline_mode=one),
                      const((1, d)),
                      const(wu_bf.shape, pipeline_mode=one),
                      const(conv_w.shape),
                      const((1, dff)),
                      const(wd_bf.shape, pipeline_mode=one),
                      const((1, d))]),
        out_specs=pl.BlockSpec((1, ts, d), lambda i, j: (i, j, 0)),
        out_shape=jax.ShapeDtypeStruct((b, s, d), F32),
        compiler_params=_cparams(("parallel", "parallel")),
        name="ffn",
    )(x, x, x, attn, attn, attn, hz, hz, hz, modr, hy_norm, w_out_bf, norm_g, wu_bf, conv_w,
      conv_b.reshape(1, dff), wd_bf, final_g)


def _rope_tables(s):
    nf = DIFF_DH // 4
    inv = ROPE_BASE ** (-jnp.arange(nf, dtype=F32) / nf)
    t = jnp.arange(s, dtype=jnp.int32)
    row = (t // GRID_W).astype(F32)[:, None] * inv
    col = (t % GRID_W).astype(F32)[:, None] * inv
    cos = jnp.concatenate([jnp.cos(row), jnp.cos(row), jnp.cos(col), jnp.cos(col)], axis=1)
    sin = jnp.concatenate([-jnp.sin(row), jnp.sin(row), -jnp.sin(col), jnp.sin(col)], axis=1)
    return jnp.tile(cos, (1, LANES // DIFF_DH)), jnp.tile(sin, (1, LANES // DIFF_DH))


def _filter_features(l):
    pos = jnp.concatenate([jnp.arange(l, dtype=jnp.int32), l - jnp.arange(l, dtype=jnp.int32)])
    valid = jnp.ones((2 * l,), F32).at[l].set(0.0)
    pos = jnp.where(pos == l, 0, pos).astype(F32)[None, :]
    tt = pos / (l - 1)
    w = 2.0 * math.pi * pos / l
    f = jnp.linspace(1e-4, HY_BANDS - 1, HY_BANDS, dtype=F32)[:, None]
    feat_t = jnp.concatenate([tt, jnp.cos(f * w), -jnp.sin(f * w)], axis=0)
    feat_t = jnp.pad(feat_t, ((0, LANES - feat_t.shape[0]), (0, 0)))
    tv = jnp.pad(jnp.stack([tt[0], valid], axis=1), ((0, 0), (0, 6)))
    return feat_t, tv


def kernel(x, c, ctx, c_ctx, w_mod, b_mod, norm_mix, norm_ffn, w_in, lam_q1, lam_k1, lam_q2, lam_k2, subln, hy_conv_w, hy_conv_b, hy_w1, hy_b1, hy_w2, hy_b2, hy_w3, hy_b3, hy_w4, hy_freq, hy_skip, hy_norm, w_out, ffn_w_up, ffn_conv_w, ffn_conv_b, ffn_w_down, final_norm):
    b, s, d = x.shape
    assert w_mod.shape[0] == 1 and b % 2 == 0 and (2 * s) % (LANES * 16) == 0
    assert s % ctx.shape[1] == 0 and ctx.shape[1] % LANES == 0
    aw = ATTN_WIDTH
    hw = hy_skip.shape[2]
    n1 = 2 * s // LANES

    rows = ((b + 1 + 7) // 8) * 8
    cvec = jnp.zeros((rows, d), F32).at[:b].set(c).at[b].set(c_ctx)
    mod = _modulation(cvec, w_mod[0], b_mod[0])[:b + 1]
    modr = jnp.pad(mod.reshape(b + 1, N_MOD, d), ((0, 0), (0, 8 - N_MOD), (0, 0)))

    w_in_bf = w_in[0].astype(BF16)
    cos_t, sin_t = _rope_tables(s)
    g_mix = norm_mix[0].reshape(1, d)
    w_vt_bf = w_in_bf[:, 2 * aw:3 * aw].T
    q, k_all, vt_all, u = _inproj(x, modr, g_mix, w_in_bf, w_vt_bf, cos_t, sin_t, hy_conv_w[0], hy_conv_b[0],
                                  ctx.shape[1])
    k_all, vt_all = _ctxproj(ctx, modr, g_mix, w_in_bf[:, aw:2 * aw], w_vt_bf, k_all, vt_all)
    lamp = jnp.concatenate([lam_q1, lam_k1, lam_q2, lam_k2], axis=0)
    attn = _attention(q, k_all, vt_all, lamp, subln[0].reshape(1, DIFF_VD))

    p = {"hy_w1": hy_w1[0], "hy_b1": hy_b1[0], "hy_w2": hy_w2[0], "hy_b2": hy_b2[0],
         "hy_w3": hy_w3[0], "hy_b3": hy_b3[0], "hy_freq": hy_freq[0]}
    fo = hy_w4.shape[1]
    w4r = hy_w4[0].reshape(fo, HY_ORDER, 2, hw).transpose(2, 0, 1, 3).reshape(2, fo, HY_ORDER * hw)
    min_decay = math.log(HY_DECAY_TARGET) / HY_FAST_DECAY_PCT
    max_decay = math.log(HY_DECAY_TARGET) / HY_SLOW_DECAY_PCT
    deltas = jnp.abs(jnp.linspace(min_decay, max_decay, HY_ORDER * 2 * hw, dtype=F32))
    deltas = deltas.reshape(HY_ORDER, 2, hw).transpose(1, 0, 2).reshape(2, 1, HY_ORDER * hw)
    kfull = _hyena_filters(*_filter_features(s), p, w4r, deltas)
    tables = _dft_tables(n1)
    kf = _filter_spectra(kfull, tables[1], tables[3], n1)

    nslab = hw // LANES
    z = _hyena_order(u, 0, u, nslab, hy_skip[0, 0:1], kf, 0, tables, n1)
    z = _hyena_order(z, 0, u, 2 * nslab, hy_skip[0, 1:2], kf, nslab, tables, n1)

    return _ffn(x, attn, z, modr, hy_norm[0].reshape(1, hw), w_out[0].astype(BF16), norm_ffn[0].reshape(1, d),
                ffn_w_up[0].astype(BF16), ffn_conv_w[0], ffn_conv_b[0], ffn_w_down[0].astype(BF16),
                final_norm.reshape(1, d))
```

```python
import functools
import math

import jax
import jax.numpy as jnp
from jax import lax
from jax.experimental import pallas as pl
from jax.experimental.pallas import tpu as pltpu

F32 = jnp.float32
BF16 = jnp.bfloat16
HIGHEST = lax.Precision.HIGHEST

EPS = 1e-6
N_MOD = 6
GRID_W = 64
DIFF_HEADS = 4
DIFF_DH = 64
DIFF_VD = 128
ATTN_WIDTH = DIFF_HEADS * DIFF_VD
HY_ORDER = 2
HY_BANDS = 16
HY_DECAY_TARGET = 1e-2
HY_FAST_DECAY_PCT = 0.3
HY_SLOW_DECAY_PCT = 1.5
ROPE_BASE = 10000.0
LAM_INIT = 0.8 - 0.6 * math.exp(-0.3 * 0)

LANES = 128
SUBLANES = 8
MXU_DIM = 256
FFN_HALO = 16
Z_PITCH = LANES + SUBLANES
DFT_STEP_ITERS = 32
DFT_UNROLL = True
VMEM_LIMIT = 56 * 1024 * 1024
MOD_COL_BLOCK = 1024
INPROJ_ROW_TILE = 1024
FFN_ROW_TILE = 512
ATTN_Q_TILE = 1024
ATTN_KV_TILES = (1408, 768, 512, 384, 256, 128)
FILTER_ROW_TILE = 1024
NEG_BIG = -1e30
NT_DIMS = (((1,), (1,)), ((), ()))
ONES_ROWS = 16
Q_SCALE = DIFF_DH ** -0.5 * math.log2(math.e)


def _cparams(sem):
    return pltpu.CompilerParams(dimension_semantics=sem, vmem_limit_bytes=VMEM_LIMIT)


def _rms(x, g):
    return x * lax.rsqrt(jnp.mean(x * x, axis=-1, keepdims=True) + EPS) * g


def _mod_kernel(c_ref, w_ref, b_ref, o_ref):
    c = c_ref[...]
    s = c * jax.nn.sigmoid(c)
    o_ref[...] = jnp.dot(s, w_ref[...], preferred_element_type=F32, precision=HIGHEST) + b_ref[...]


def _modulation(cvec, w_mod, b_mod):
    rows, d = cvec.shape
    cols = w_mod.shape[1]
    bc = MOD_COL_BLOCK
    return pl.pallas_call(
        _mod_kernel,
        grid=(cols // bc,),
        in_specs=[pl.BlockSpec((rows, d), lambda j: (0, 0)),
                  pl.BlockSpec((d, bc), lambda j: (0, j)),
                  pl.BlockSpec((1, bc), lambda j: (0, j))],
        out_specs=pl.BlockSpec((rows, bc), lambda j: (0, j)),
        out_shape=jax.ShapeDtypeStruct((rows, cols), F32),
        compiler_params=_cparams(("arbitrary",)),
        name="mod",
    )(cvec, w_mod, b_mod.reshape(1, cols))


def _rope(t, cos, sin_signed, first_half):
    sw = jnp.where(first_half, pltpu.roll(t, LANES - 16, 1), pltpu.roll(t, 16, 1))
    return t * cos + sw * sin_signed


def _halo_specs(ts, s, d, halo):
    nh = ts // halo
    nblk = s // halo
    return [pl.BlockSpec((1, ts, d), lambda i, j: (i, j, 0)),
            pl.BlockSpec((1, halo, d), lambda i, j: (i, jnp.maximum(j * nh - 1, 0), 0)),
            pl.BlockSpec((1, halo, d), lambda i, j: (i, jnp.minimum((j + 1) * nh, nblk - 1), 0))]


def _slab_halo_specs(ts, s, nslab, halo):
    nh = ts // halo
    nblk = s // halo
    return [pl.BlockSpec((1, nslab, ts, LANES), lambda i, j: (i, 0, j, 0)),
            pl.BlockSpec((1, nslab, halo, LANES), lambda i, j: (i, 0, jnp.maximum(j * nh - 1, 0), 0)),
            pl.BlockSpec((1, nslab, halo, LANES), lambda i, j: (i, 0, jnp.minimum((j + 1) * nh, nblk - 1), 0))]


def _with_halo(ref, prev_ref, next_ref):
    return jnp.concatenate([prev_ref[0], ref[0], next_ref[0]], axis=0)


def _conv3(a, w, bias, ts, halo, no_prev, no_next):
    rows = ts + 2 * halo
    prev = jnp.where(no_prev, 0.0, pltpu.roll(a, 1, 0)[halo:halo + ts])
    nxt = jnp.where(no_next, 0.0, pltpu.roll(a, rows - 1, 0)[halo:halo + ts])
    return prev * w[0:1] + a[halo:halo + ts] * w[1:2] + nxt * w[2:3] + bias


def _edge_masks(ts):
    j = pl.program_id(1)
    row = lax.broadcasted_iota(jnp.int32, (ts, 1), 0)
    no_prev = jnp.logical_and(j == 0, row == 0)
    no_next = jnp.logical_and(j == pl.num_programs(1) - 1, row == ts - 1)
    return no_prev, no_next


def _inproj_kernel(x_ref, xp_ref, xn_ref, mod_ref, g_ref, w_ref, wvt_ref, cos_ref, sin_ref, cw_ref, cb_ref,
                   q_ref, k_ref, vt_ref, u_ref):
    ts = x_ref.shape[1]
    m = mod_ref[0]
    xa = _with_halo(x_ref, xp_ref, xn_ref)
    ha = (_rms(xa, g_ref[...]) * (1.0 + m[1:2]) + m[0:1]).astype(BF16)
    h = ha[SUBLANES:SUBLANES + ts]
    cos = cos_ref[...]
    sin = sin_ref[...]
    lane = lax.broadcasted_iota(jnp.int32, cos.shape, 1)
    first_half = (lane % 32) < 16
    aw = ATTN_WIDTH
    q = jnp.dot(h, w_ref[:, :aw], preferred_element_type=F32)
    k = jnp.dot(h, w_ref[:, aw:2 * aw], preferred_element_type=F32)
    for s in range(aw // LANES):
        c0 = s * LANES
        q_ref[0, :, c0:c0 + LANES] = (_rope(q[:, c0:c0 + LANES], cos, sin, first_half) * Q_SCALE).astype(BF16)
        k_ref[0, :, c0:c0 + LANES] = _rope(k[:, c0:c0 + LANES], cos, sin, first_half).astype(BF16)
    vt_ref[0] = lax.dot_general(wvt_ref[...], h, NT_DIMS, preferred_element_type=F32).astype(BF16)
    hy = jnp.dot(ha, w_ref[:, 3 * aw:], preferred_element_type=F32)
    u = _conv3(hy, cw_ref[...], cb_ref[...], ts, SUBLANES, *_edge_masks(ts)).astype(BF16)
    for sl in range(u_ref.shape[1]):
        u_ref[0, sl] = u[:, sl * LANES:(sl + 1) * LANES]


def _inproj(x, modr, norm_g, w_in_bf, w_vt_bf, cos_t, sin_t, conv_w, conv_b, ctx_len):
    b, s, d = x.shape
    ts = min(INPROJ_ROW_TILE, s)
    aw = ATTN_WIDTH
    hyc = w_in_bf.shape[1] - 3 * aw
    sk = s + ctx_len
    return pl.pallas_call(
        _inproj_kernel,
        grid=(b, s // ts),
        in_specs=_halo_specs(ts, s, d, SUBLANES) + [
                  pl.BlockSpec((1, SUBLANES, d), lambda i, j: (i, 0, 0)),
                  pl.BlockSpec((1, d), lambda i, j: (0, 0)),
                  pl.BlockSpec(w_in_bf.shape, lambda i, j: (0, 0)),
                  pl.BlockSpec(w_vt_bf.shape, lambda i, j: (0, 0)),
                  pl.BlockSpec((ts, LANES), lambda i, j: (j, 0)),
                  pl.BlockSpec((ts, LANES), lambda i, j: (j, 0)),
                  pl.BlockSpec(conv_w.shape, lambda i, j: (0, 0)),
                  pl.BlockSpec((1, hyc), lambda i, j: (0, 0))],
        out_specs=[pl.BlockSpec((1, ts, aw), lambda i, j: (i, j, 0)),
                   pl.BlockSpec((1, ts, aw), lambda i, j: (i, j, 0)),
                   pl.BlockSpec((1, aw, ts), lambda i, j: (i, 0, j)),
                   pl.BlockSpec((1, hyc // LANES, ts, LANES), lambda i, j: (i, 0, j, 0))],
        out_shape=[jax.ShapeDtypeStruct((b, s, aw), BF16), jax.ShapeDtypeStruct((b, sk, aw), BF16),
                   jax.ShapeDtypeStruct((b, aw, sk), BF16),
                   jax.ShapeDtypeStruct((b, hyc // LANES, s, LANES), BF16)],
        compiler_params=_cparams(("parallel", "parallel")),
        name="inproj",
    )(x, x, x, modr, norm_g, w_in_bf, w_vt_bf, cos_t, sin_t, conv_w, conv_b.reshape(1, hyc))


def _ctxproj_kernel(x_ref, mod_ref, g_ref, wk_ref, wvt_ref, k_in_ref, vt_in_ref, k_ref, vt_ref):
    del k_in_ref, vt_in_ref
    m = mod_ref[0]
    h = (_rms(x_ref[0], g_ref[...]) * (1.0 + m[1:2]) + m[0:1]).astype(BF16)
    k_ref[0] = jnp.dot(h, wk_ref[...], preferred_element_type=F32).astype(BF16)
    vt_ref[0] = lax.dot_general(wvt_ref[...], h, NT_DIMS, preferred_element_type=F32).astype(BF16)


def _ctxproj(ctx, modr, norm_g, w_k_bf, w_vt_bf, k_all, vt_all):
    b, c, d = ctx.shape
    aw = ATTN_WIDTH
    nb = modr.shape[0] - 1
    blk = (k_all.shape[1] - c) // c
    return pl.pallas_call(
        _ctxproj_kernel,
        grid=(b,),
        in_specs=[pl.BlockSpec((1, c, d), lambda i: (i, 0, 0)),
                  pl.BlockSpec((1, SUBLANES, d), lambda i: (nb, 0, 0)),
                  pl.BlockSpec((1, d), lambda i: (0, 0)),
                  pl.BlockSpec(w_k_bf.shape, lambda i: (0, 0)),
                  pl.BlockSpec(w_vt_bf.shape, lambda i: (0, 0)),
                  pl.BlockSpec(memory_space=pl.ANY),
                  pl.BlockSpec(memory_space=pl.ANY)],
        out_specs=[pl.BlockSpec((1, c, aw), lambda i: (i, blk, 0)), pl.BlockSpec((1, aw, c), lambda i: (i, 0, blk))],
        out_shape=[jax.ShapeDtypeStruct(k_all.shape, BF16), jax.ShapeDtypeStruct(vt_all.shape, BF16)],
        input_output_aliases={5: 0, 6: 1},
        compiler_params=_cparams(("parallel",)),
        name="ctxproj",
    )(ctx, modr, norm_g, w_k_bf, w_vt_bf, k_all, vt_all)


def _attn_kernel(q_ref, k_ref, vt_ref, lam_ref, sub_ref, o_ref, qm_scr, sa_scr, sb_scr, acc_scr,
                 *, tq, tk, nq, nkv):
    q = q_ref[0]
    lane = lax.broadcasted_iota(jnp.int32, q.shape, 1)
    zero = jnp.zeros_like(q)
    qm_scr[0] = jnp.where(lane < DIFF_DH, q, zero)
    qm_scr[1] = jnp.where(lane >= DIFF_DH, q, zero)
    acc_scr[...] = jnp.zeros(acc_scr.shape, F32)
    ones = jnp.ones((ONES_ROWS, tk), BF16)
    steps = nq * nkv
    neg = jnp.full((1, tq), NEG_BIG, F32)
    lp = lam_ref[...]
    lam = (jnp.exp(jnp.sum(lp[0:1] * lp[1:2], axis=-1, keepdims=True))
           - jnp.exp(jnp.sum(lp[2:3] * lp[3:4], axis=-1, keepdims=True)) + LAM_INIT)

    def restart(t, m):
        return [jnp.where(t % nkv == 0, neg, mm) for mm in m]

    def scores(t, s_scr, m_cur):
        t = jnp.minimum(t, steps - 1)
        q0 = pl.multiple_of((t // nkv) * tq, tq)
        kj = k_ref[0, pl.ds(pl.multiple_of((t % nkv) * tk, tk), tk), :]
        m_base = restart(t, m_cur)
        m_next = []
        for mp in range(2):
            s = lax.dot_general(kj, qm_scr[mp, pl.ds(q0, tq), :], NT_DIMS, preferred_element_type=F32)
            s_scr[mp] = s
            m_next.append(jnp.maximum(m_base[mp], jnp.max(s, axis=0, keepdims=True)))
        return m_next

    def accumulate(t, s_scr, m_prev, m_cur):
        j = t % nkv
        vt = vt_ref[0, :, pl.ds(pl.multiple_of(j * tk, tk), tk)]
        vext = jnp.concatenate([vt, ones], axis=0)
        m_old = restart(t, m_prev)
        for mp in range(2):
            p = jnp.exp2((s_scr[mp] - m_cur[mp]).astype(BF16))
            alpha = jnp.exp2(m_old[mp] - m_cur[mp])
            acc_scr[mp] = alpha * acc_scr[mp] + jnp.dot(vext, p, preferred_element_type=F32)

        @pl.when(j == nkv - 1)
        def _():
            a0 = acc_scr[0]
            a1 = acc_scr[1]
            vd = DIFF_VD
            o = a0[:vd] / a0[vd:vd + 1] - lam * (a1[:vd] / a1[vd:vd + 1])
            on = o * lax.rsqrt(jnp.mean(o * o, axis=0, keepdims=True) + EPS)
            q0 = pl.multiple_of((t // nkv) * tq, tq)
            o_ref[0, pl.ds(q0, tq), :] = (on.T * sub_ref[...] * (1.0 - LAM_INIT)).astype(BF16)

    m_first = scores(0, sa_scr, [neg, neg])

    def pair(i, carry):
        m_prev, m_cur = list(carry[:2]), list(carry[2:])
        t = 2 * i
        m_nxt = scores(t + 1, sb_scr, m_cur)
        accumulate(t, sa_scr, m_prev, m_cur)
        m_nn = scores(t + 2, sa_scr, m_nxt)
        accumulate(t + 1, sb_scr, m_cur, m_nxt)
        return (*m_nxt, *m_nn)

    carry = lax.fori_loop(0, steps // 2, pair, (neg, neg, *m_first))
    if steps % 2:
        accumulate(steps - 1, sa_scr, list(carry[:2]), list(carry[2:]))


def _pick_tile(n, cands):
    for c in cands:
        if n % c == 0:
            return c
    raise ValueError(f"no tile for {n}")


def _attention(q, k_all, vt_all, lamp, subln):
    b, s, aw = q.shape
    sk = k_all.shape[1]
    tq = min(ATTN_Q_TILE, s)
    tk = _pick_tile(sk, ATTN_KV_TILES)
    kern = functools.partial(_attn_kernel, tq=tq, tk=tk, nq=s // tq, nkv=sk // tk)
    return pl.pallas_call(
        kern,
        grid=(b, DIFF_HEADS),
        in_specs=[pl.BlockSpec((1, s, LANES), lambda i, h: (i, 0, h)),
                  pl.BlockSpec((1, sk, LANES), lambda i, h: (i, 0, h)),
                  pl.BlockSpec((1, DIFF_VD, sk), lambda i, h: (i, h, 0)),
                  pl.BlockSpec(lamp.shape, lambda i, h: (0, 0)),
                  pl.BlockSpec((1, DIFF_VD), lambda i, h: (0, 0))],
        out_specs=pl.BlockSpec((1, s, LANES), lambda i, h: (i, 0, h)),
        out_shape=jax.ShapeDtypeStruct((b, s, aw), BF16),
        scratch_shapes=[pltpu.VMEM((2, s, LANES), BF16),
                        pltpu.VMEM((2, tk, tq), F32),
                        pltpu.VMEM((2, tk, tq), F32),
                        pltpu.VMEM((2, DIFF_VD + ONES_ROWS, tq), F32)],
        compiler_params=_cparams(("parallel", "parallel")),
        name="attn",
    )(q, k_all, vt_all, lamp, subln)


def _filt_kernel(ft_ref, tv_ref, w1t_ref, b1_ref, w2t_ref, b2_ref, w3t_ref, b3_ref, fr_ref, w4_ref, dl_ref, o_ref):
    dot = functools.partial(jnp.dot, preferred_element_type=F32, precision=HIGHEST)
    fr = fr_ref[...]
    h = jnp.sin(fr * (dot(w1t_ref[...], ft_ref[...]) + b1_ref[...]))
    h = jnp.sin(fr * (dot(w2t_ref[...], h) + b2_ref[...]))
    h = jnp.sin(fr * (dot(w3t_ref[...], h) + b3_ref[...]))
    k = lax.dot_general(h, w4_ref[0], (((0,), (0,)), ((), ())), preferred_element_type=F32,
                        precision=HIGHEST)
    tv = tv_ref[...]
    o_ref[...] = k * jnp.exp(-tv[:, 0:1] * dl_ref[0]) * tv[:, 1:2]


def _hyena_filters(feat_t, tv, p, w4r, deltas):
    n = feat_t.shape[1]
    tr = min(FILTER_ROW_TILE, n // 2)
    half_steps = (n // 2) // tr
    fo = p["hy_w2"].shape[0]
    cols = w4r.shape[2]
    const = lambda shape: pl.BlockSpec(shape, lambda i: (0,) * len(shape))
    w1t = jnp.zeros((fo, LANES), F32).at[:, :p["hy_w1"].shape[0]].set(p["hy_w1"].T)
    col = lambda v: v.reshape(fo, 1)
    return pl.pallas_call(
        _filt_kernel,
        grid=(n // tr,),
        in_specs=[pl.BlockSpec((LANES, tr), lambda i: (0, i)),
                  pl.BlockSpec((tr, tv.shape[1]), lambda i: (i, 0)),
                  const((fo, LANES)), const((fo, 1)), const((fo, fo)), const((fo, 1)),
                  const((fo, fo)), const((fo, 1)), const((fo, 1)),
                  pl.BlockSpec((1, fo, cols), lambda i: (i // half_steps, 0, 0)),
                  pl.BlockSpec((1, 1, cols), lambda i: (i // half_steps, 0, 0))],
        out_specs=pl.BlockSpec((tr, cols), lambda i: (i, 0)),
        out_shape=jax.ShapeDtypeStruct((n, cols), F32),
        compiler_params=_cparams(("parallel",)),
        name="filt",
    )(feat_t, tv, w1t, col(p["hy_b1"]), p["hy_w2"].T, col(p["hy_b2"]), p["hy_w3"].T, col(p["hy_b3"]),
      col(p["hy_freq"]), w4r, deltas)


def _dft_tables(n1):
    n2 = LANES
    n = n1 * n2
    h1 = n1 // 2
    i2 = jnp.arange(n2, dtype=jnp.int32)
    i1 = jnp.arange(n1, dtype=jnp.int32)
    at = ((i2[:, None] * i1[None, :]) % n).astype(F32) * (-2.0 * math.pi / n)
    a1 = ((i1[:, None] * i1[None, :]) % n1).astype(F32) * (-2.0 * math.pi / n1)
    tr, ti = jnp.cos(at)[:, :, None], jnp.sin(at)[:, :, None]
    f1r, f1i = jnp.cos(a1)[None], jnp.sin(a1)[None]
    gr, gi = tr * f1r - ti * f1i, tr * f1i + ti * f1r
    grh, gih = gr[:, :, :h1], gi[:, :, :h1]
    g1_sig = jnp.concatenate([jnp.concatenate([grh, -gih], 2), jnp.concatenate([gih, grh], 2)], 1)
    g1_flt = jnp.concatenate([gr, gi], 1)
    qr, qi = jnp.swapaxes(grh, 1, 2), -jnp.swapaxes(gih, 1, 2)
    g3 = jnp.concatenate([jnp.concatenate([qr, -qi], 2), jnp.concatenate([qi, qr], 2)], 1)
    a2 = ((i2[:, None] * i2[None, :]) % n2).astype(F32) * (-2.0 * math.pi / n2)
    fr, fi = jnp.cos(a2), jnp.sin(a2)
    f2 = jnp.concatenate([jnp.concatenate([fr, -fi], 1), jnp.concatenate([fi, fr], 1)], 0)
    f2i = jnp.concatenate([jnp.concatenate([fr, fi], 1), jnp.concatenate([-fi, fr], 1)], 0)
    return (g1_sig.astype(BF16), g1_flt.astype(BF16), g3.astype(BF16), f2.astype(BF16), f2i.astype(BF16))


def _a_pitch(n1):
    return 2 * n1 + SUBLANES


def _stage1(zy_ref, a_ref, g_ref, step, it, n1):
    pa = _a_pitch(n1)

    def body(i, c):
        n2 = step * it + i
        rhs = zy_ref[pl.ds(n2, n1, stride=Z_PITCH), :].astype(BF16)
        a_ref[pl.ds(pl.multiple_of(n2 * pa, SUBLANES), 2 * n1), :] = jnp.dot(
            g_ref[i], rhs, preferred_element_type=F32)
        return c

    lax.fori_loop(0, it, body, 0, unroll=DFT_UNROLL)


def _gather_k1(a_ref, k1, n1):
    pa = _a_pitch(n1)
    re = a_ref[pl.ds(k1, LANES, stride=pa), :]
    im = a_ref[pl.ds(n1 + k1, LANES, stride=pa), :]
    return jnp.concatenate([re, im], axis=0).astype(BF16)


def _load_blocks(dst_ref, src, nblk, base):
    def body(i, c):
        r0 = pl.multiple_of(i * LANES, LANES)
        d0 = pl.multiple_of((base + i) * Z_PITCH, SUBLANES)
        dst_ref[pl.ds(d0, LANES), :] = src(r0).astype(F32)
        return c

    lax.fori_loop(0, nblk, body, 0)


def _fdft_kernel(k_ref, g1_ref, f2_ref, o_ref, zy_ref, a_ref, *, n1, it1, itm, s1):
    t = pl.program_id(1)

    @pl.when(t == 0)
    def _():
        _load_blocks(zy_ref, lambda r0: k_ref[pl.ds(r0, LANES), :], n1, 0)

    @pl.when(t < s1)
    def _():
        _stage1(zy_ref, a_ref, g1_ref, t, it1, n1)

    @pl.when(t >= s1)
    def _():
        f2 = f2_ref[...]
        scale = 1.0 / (n1 * LANES)

        def body(i, c):
            k1 = (t - s1) * itm + i
            y = jnp.dot(f2, _gather_k1(a_ref, k1, n1), preferred_element_type=F32)
            o_ref[0, i] = (y * scale).astype(BF16)
            return c

        lax.fori_loop(0, itm, body, 0, unroll=DFT_UNROLL)


def _filter_spectra(kfull, g1_flt, f2, n1):
    n, cols = kfull.shape
    nslab = cols // LANES
    it1 = DFT_STEP_ITERS
    itm = min(DFT_STEP_ITERS, n1)
    s1 = LANES // it1
    sm = n1 // itm
    kern = functools.partial(_fdft_kernel, n1=n1, it1=it1, itm=itm, s1=s1)
    return pl.pallas_call(
        kern,
        grid=(nslab, s1 + sm),
        in_specs=[pl.BlockSpec((n, LANES), lambda s, t: (0, s)),
                  pl.BlockSpec((it1, 2 * n1, n1), lambda s, t: (jnp.minimum(t, s1 - 1), 0, 0)),
                  pl.BlockSpec((2 * LANES, 2 * LANES), lambda s, t: (0, 0))],
        out_specs=pl.BlockSpec((1, itm, 2 * LANES, LANES), lambda s, t: (s, jnp.maximum(t - s1, 0), 0, 0)),
        out_shape=jax.ShapeDtypeStruct((nslab, n1, 2 * LANES, LANES), BF16),
        scratch_shapes=[pltpu.VMEM((n1 * Z_PITCH, LANES), F32),
                        pltpu.VMEM((LANES * _a_pitch(n1), LANES), F32)],
        compiler_params=_cparams(("parallel", "arbitrary")),
        name="fdft",
    )(kfull, g1_flt, f2)


def _hyena_kernel(z_ref, gate_ref, skip_ref, g1_ref, kf_ref, g3_ref, f2_ref, f2i_ref, o_ref,
                  zy_ref, a_ref, *, n1, it1, itm, s1, sm):
    t = pl.program_id(1)
    h1 = n1 // 2
    pa = _a_pitch(n1)

    @pl.when(t == 0)
    def _():
        for bb in range(2):
            _load_blocks(zy_ref, lambda r0, bb=bb: z_ref[bb, 0, pl.ds(r0, LANES), :], h1, bb * h1)

    @pl.when(t < s1)
    def _():
        _stage1(zy_ref, a_ref, g1_ref, t, it1, n1)

    @pl.when(jnp.logical_and(t >= s1, t < s1 + sm))
    def _():
        f2 = f2_ref[...]
        f2i = f2i_ref[...]

        def body(i, c):
            k1 = (t - s1) * itm + i
            y = jnp.dot(f2, _gather_k1(a_ref, k1, n1), preferred_element_type=F32)
            kf = kf_ref[0, i].astype(F32)
            yr, yi = y[:LANES], y[LANES:]
            kr, ki = kf[:LANES], kf[LANES:]
            prod = jnp.concatenate([yr * kr - yi * ki, yr * ki + yi * kr], axis=0).astype(BF16)
            r = jnp.dot(f2i, prod, preferred_element_type=F32)
            a_ref[pl.ds(k1, LANES, stride=pa), :] = r[:LANES]
            a_ref[pl.ds(n1 + k1, LANES, stride=pa), :] = r[LANES:]
            return c

        lax.fori_loop(0, itm, body, 0, unroll=DFT_UNROLL)

    @pl.when(t >= s1 + sm)
    def _():
        def body(i, c):
            n2 = (t - s1 - sm) * it1 + i
            rhs = a_ref[pl.ds(pl.multiple_of(n2 * pa, SUBLANES), 2 * n1), :].astype(BF16)
            zy_ref[pl.ds(n2, n1, stride=Z_PITCH), :] = jnp.dot(g3_ref[i], rhs, preferred_element_type=F32)
            return c

        lax.fori_loop(0, it1, body, 0, unroll=DFT_UNROLL)

    @pl.when(t == s1 + sm + s1 - 1)
    def _():
        skip = skip_ref[...]
        for bb in range(2):
            def body(i, c, bb=bb):
                r0 = pl.multiple_of(i * LANES, LANES)
                d0 = pl.multiple_of((bb * h1 + i) * Z_PITCH, SUBLANES)
                y = zy_ref[pl.ds(d0, LANES), :]
                zz = z_ref[bb, 0, pl.ds(r0, LANES), :].astype(F32)
                gg = gate_ref[bb, 0, pl.ds(r0, LANES), :].astype(F32)
                o_ref[bb, 0, pl.ds(r0, LANES), :] = (gg * (y + skip * zz)).astype(BF16)
                return c

            lax.fori_loop(0, h1, body, 0)


def _hyena_order(z, z_off, gate, gate_off, skip, kf, kf_off, tables, n1):
    g1_sig, _, g3, f2, f2i = tables
    b, _, s, _ = z.shape
    width = skip.shape[1]
    nslab = width // LANES
    it1 = DFT_STEP_ITERS
    itm = min(DFT_STEP_ITERS, n1)
    s1 = LANES // it1
    sm = n1 // itm
    steps = 2 * s1 + sm
    kern = functools.partial(_hyena_kernel, n1=n1, it1=it1, itm=itm, s1=s1, sm=sm)
    one = pl.Buffered(1)
    return pl.pallas_call(
        kern,
        grid=((b // 2) * nslab, steps),
        in_specs=[pl.BlockSpec((2, 1, s, LANES), lambda g, t: (g // nslab, z_off + g % nslab, 0, 0),
                               pipeline_mode=one),
                  pl.BlockSpec((2, 1, s, LANES), lambda g, t: (g // nslab, gate_off + g % nslab, 0, 0),
                               pipeline_mode=one),
                  pl.BlockSpec((1, LANES), lambda g, t: (0, g % nslab)),
                  pl.BlockSpec((it1, 2 * n1, n1), lambda g, t: (jnp.minimum(t, s1 - 1), 0, 0)),
                  pl.BlockSpec((1, itm, 2 * LANES, LANES),
                               lambda g, t: (kf_off + g % nslab, jnp.clip(t - s1, 0, sm - 1), 0, 0)),
                  pl.BlockSpec((it1, n1, 2 * n1), lambda g, t: (jnp.clip(t - s1 - sm, 0, s1 - 1), 0, 0)),
                  pl.BlockSpec((2 * LANES, 2 * LANES), lambda g, t: (0, 0)),
                  pl.BlockSpec((2 * LANES, 2 * LANES), lambda g, t: (0, 0))],
        out_specs=pl.BlockSpec((2, 1, s, LANES), lambda g, t: (g // nslab, g % nslab, 0, 0)),
        out_shape=jax.ShapeDtypeStruct((b, nslab, s, LANES), BF16),
        scratch_shapes=[pltpu.VMEM((n1 * Z_PITCH, LANES), F32),
                        pltpu.VMEM((LANES * _a_pitch(n1), LANES), F32)],
        compiler_params=_cparams(("parallel", "arbitrary")),
        name="hyena",
    )(z, gate, skip, g1_sig, kf, g3, f2, f2i)


def _ffn_kernel(x_ref, xp_ref, xn_ref, a_ref, ap_ref, an_ref, z_ref, zp_ref, zn_ref, mod_ref, hn_ref, wo_ref,
                g_ref, wu_ref, cw_ref, cb_ref, wd_ref, fg_ref, o_ref, *, dff, chunks):
    ts = x_ref.shape[1]
    m = mod_ref[0]
    aw = ATTN_WIDTH
    za = jnp.concatenate([jnp.concatenate([zp_ref[0, sl], z_ref[0, sl], zn_ref[0, sl]], axis=0)
                          for sl in range(z_ref.shape[1])], axis=1)
    zn = _rms(za.astype(F32), hn_ref[...]).astype(BF16)
    mix = (jnp.dot(_with_halo(a_ref, ap_ref, an_ref), wo_ref[:aw, :], preferred_element_type=F32)
           + jnp.dot(zn, wo_ref[aw:, :], preferred_element_type=F32))
    x1 = _with_halo(x_ref, xp_ref, xn_ref) + m[2:3] * mix
    h = (_rms(x1, g_ref[...]) * (1.0 + m[4:5]) + m[3:4]).astype(BF16)
    hm = h[FFN_HALO:FFN_HALO + ts]
    masks = _edge_masks(ts)
    acc = jnp.zeros((ts, x1.shape[1]), F32)
    for c0, cf in chunks:
        a = jnp.dot(h, wu_ref[:, c0:c0 + cf], preferred_element_type=F32)
        gate = jnp.dot(hm, wu_ref[:, dff + c0:dff + c0 + cf], preferred_element_type=F32)
        ac = _conv3(a, cw_ref[:, c0:c0 + cf], cb_ref[:, c0:c0 + cf], ts, FFN_HALO, *masks)
        gl = 0.5 * ac * (1.0 + lax.erf(ac * (2.0 ** -0.5)))
        acc = acc + jnp.dot((gl * gate).astype(BF16), wd_ref[c0:c0 + cf, :], preferred_element_type=F32)
    y = x1[FFN_HALO:FFN_HALO + ts] + m[5:6] * acc
    o_ref[0] = _rms(y, fg_ref[...])


def _mxu_chunks(n, parts):
    tiles = n // MXU_DIM
    if tiles < parts or n % MXU_DIM:
        return ((0, n),)
    sizes = [(tiles // parts + (1 if i < tiles % parts else 0)) * MXU_DIM for i in range(parts)]
    return tuple((sum(sizes[:i]), sizes[i]) for i in range(parts))


def _ffn(x, attn, hz, modr, hy_norm, w_out_bf, norm_g, wu_bf, conv_w, conv_b, wd_bf, final_g):
    b, s, d = x.shape
    dff = wd_bf.shape[0]
    ts = min(FFN_ROW_TILE, s)
    kern = functools.partial(_ffn_kernel, dff=dff, chunks=_mxu_chunks(dff, 2))
    one = pl.Buffered(1)
    const = lambda shape, **kw: pl.BlockSpec(shape, lambda i, j: (0,) * len(shape), **kw)
    return pl.pallas_call(
        kern,
        grid=(b, s // ts),
        in_specs=(_halo_specs(ts, s, d, FFN_HALO) + _halo_specs(ts, s, attn.shape[2], FFN_HALO)
                  + _slab_halo_specs(ts, s, hz.shape[1], FFN_HALO) + [
                      pl.BlockSpec((1, SUBLANES, d), lambda i, j: (i, 0, 0)),
                      const(hy_norm.shape),
                      const(w_out_bf.shape, pipeline_mode=one),
                      const((1, d)),
                      const(wu_bf.shape, pipeline_mode=one),
                      const(conv_w.shape),
                      const((1, dff)),
                      const(wd_bf.shape, pipeline_mode=one),
                      const((1, d))]),
        out_specs=pl.BlockSpec((1, ts, d), lambda i, j: (i, j, 0)),
        out_shape=jax.ShapeDtypeStruct((b, s, d), F32),
        compiler_params=_cparams(("parallel", "parallel")),
        name="ffn",
    )(x, x, x, attn, attn, attn, hz, hz, hz, modr, hy_norm, w_out_bf, norm_g, wu_bf, conv_w,
      conv_b.reshape(1, dff), wd_bf, final_g)


def _rope_tables(s):
    nf = DIFF_DH // 4
    inv = ROPE_BASE ** (-jnp.arange(nf, dtype=F32) / nf)
    t = jnp.arange(s, dtype=jnp.int32)
    row = (t // GRID_W).astype(F32)[:, None] * inv
    col = (t % GRID_W).astype(F32)[:, None] * inv
    cos = jnp.concatenate([jnp.cos(row), jnp.cos(row), jnp.cos(col), jnp.cos(col)], axis=1)
    sin = jnp.concatenate([-jnp.sin(row), jnp.sin(row), -jnp.sin(col), jnp.sin(col)], axis=1)
    return jnp.tile(cos, (1, LANES // DIFF_DH)), jnp.tile(sin, (1, LANES // DIFF_DH))


def _filter_features(l):
    pos = jnp.concatenate([jnp.arange(l, dtype=jnp.int32), l - jnp.arange(l, dtype=jnp.int32)])
    valid = jnp.ones((2 * l,), F32).at[l].set(0.0)
    pos = jnp.where(pos == l, 0, pos).astype(F32)[None, :]
    tt = pos / (l - 1)
    w = 2.0 * math.pi * pos / l
    f = jnp.linspace(1e-4, HY_BANDS - 1, HY_BANDS, dtype=F32)[:, None]
    feat_t = jnp.concatenate([tt, jnp.cos(f * w), -jnp.sin(f * w)], axis=0)
    feat_t = jnp.pad(feat_t, ((0, LANES - feat_t.shape[0]), (0, 0)))
    tv = jnp.pad(jnp.stack([tt[0], valid], axis=1), ((0, 0), (0, SUBLANES - 2)))
    return feat_t, tv


def kernel(x, c, ctx, c_ctx, w_mod, b_mod, norm_mix, norm_ffn, w_in, lam_q1, lam_k1, lam_q2, lam_k2, subln, hy_conv_w, hy_conv_b, hy_w1, hy_b1, hy_w2, hy_b2, hy_w3, hy_b3, hy_w4, hy_freq, hy_skip, hy_norm, w_out, ffn_w_up, ffn_conv_w, ffn_conv_b, ffn_w_down, final_norm):
    b, s, d = x.shape
    assert w_mod.shape[0] == 1 and b % 2 == 0 and (2 * s) % (LANES * 16) == 0
    assert s % ctx.shape[1] == 0 and ctx.shape[1] % LANES == 0
    aw = ATTN_WIDTH
    hw = hy_skip.shape[2]
    n1 = 2 * s // LANES

    rows = -(-(b + 1) // SUBLANES) * SUBLANES
    cvec = jnp.zeros((rows, d), F32).at[:b].set(c).at[b].set(c_ctx)
    mod = _modulation(cvec, w_mod[0], b_mod[0])[:b + 1]
    modr = jnp.pad(mod.reshape(b + 1, N_MOD, d), ((0, 0), (0, SUBLANES - N_MOD), (0, 0)))

    w_in_bf = w_in[0].astype(BF16)
    cos_t, sin_t = _rope_tables(s)
    g_mix = norm_mix[0].reshape(1, d)
    w_vt_bf = w_in_bf[:, 2 * aw:3 * aw].T
    q, k_all, vt_all, u = _inproj(x, modr, g_mix, w_in_bf, w_vt_bf, cos_t, sin_t, hy_conv_w[0], hy_conv_b[0],
                                  ctx.shape[1])
    k_all, vt_all = _ctxproj(ctx, modr, g_mix, w_in_bf[:, aw:2 * aw], w_vt_bf, k_all, vt_all)
    lamp = jnp.concatenate([lam_q1, lam_k1, lam_q2, lam_k2], axis=0)
    attn = _attention(q, k_all, vt_all, lamp, subln[0].reshape(1, DIFF_VD))

    p = {"hy_w1": hy_w1[0], "hy_b1": hy_b1[0], "hy_w2": hy_w2[0], "hy_b2": hy_b2[0],
         "hy_w3": hy_w3[0], "hy_b3": hy_b3[0], "hy_freq": hy_freq[0]}
    fo = hy_w4.shape[1]
    w4r = hy_w4[0].reshape(fo, HY_ORDER, 2, hw).transpose(2, 0, 1, 3).reshape(2, fo, HY_ORDER * hw)
    min_decay = math.log(HY_DECAY_TARGET) / HY_FAST_DECAY_PCT
    max_decay = math.log(HY_DECAY_TARGET) / HY_SLOW_DECAY_PCT
    deltas = jnp.abs(jnp.linspace(min_decay, max_decay, HY_ORDER * 2 * hw, dtype=F32))
    deltas = deltas.reshape(HY_ORDER, 2, hw).transpose(1, 0, 2).reshape(2, 1, HY_ORDER * hw)
    kfull = _hyena_filters(*_filter_features(s), p, w4r, deltas)
    tables = _dft_tables(n1)
    kf = _filter_spectra(kfull, tables[1], tables[3], n1)

    nslab = hw // LANES
    z = _hyena_order(u, 0, u, nslab, hy_skip[0, 0:1], kf, 0, tables, n1)
    z = _hyena_order(z, 0, u, 2 * nslab, hy_skip[0, 1:2], kf, nslab, tables, n1)

    return _ffn(x, attn, z, modr, hy_norm[0].reshape(1, hw), w_out[0].astype(BF16), norm_ffn[0].reshape(1, d),
                ffn_w_up[0].astype(BF16), ffn_conv_w[0], ffn_conv_b[0], ffn_w_down[0].astype(BF16),
                final_norm.reshape(1, d))
```
